```python
import jax
import jax.numpy as jnp
from jax import lax
import numpy as np

D_MODEL = 1024
BATCH = 8
SEQ = 4096
DEPTH = 2

GRID_W = 64
CTX_LEN = 256
D_MIX = D_MODEL
CONV_WIDTH = D_MIX // 4
CONV_KERNEL = 31
ATTN_WIDTH = D_MIX // 2
ATTN_HEAD_DIM = 64
ATTN_HEADS = ATTN_WIDTH // ATTN_HEAD_DIM
ATTN_KV_HEADS = 2
ATTN_GROUP = ATTN_HEADS // ATTN_KV_HEADS
ATTN_WINDOW = 128
ATTN_BLOCK = 128
ROPE_THETA = 10000.0
GLA_WIDTH = D_MIX - CONV_WIDTH - ATTN_WIDTH
GLA_HEADS = 4
GLA_DV = GLA_WIDTH // GLA_HEADS
GLA_DK = GLA_DV // 2
GLA_LOW_RANK = 16
GLA_TAU = 16.0
GLA_CHUNK = 64
NORM_EPS = 1e-6
NEG_INF = -1e30
F32 = jnp.float32

IN_COLUMNS = (
    ('a_val', CONV_WIDTH), ('a_glu', CONV_WIDTH), ('a_gate', CONV_WIDTH),
    ('b_q', ATTN_WIDTH), ('b_k', ATTN_KV_HEADS * ATTN_HEAD_DIM), ('b_v', ATTN_KV_HEADS * ATTN_HEAD_DIM), ('b_gate', ATTN_WIDTH),
    ('c_q', GLA_HEADS * GLA_DK), ('c_k', GLA_HEADS * GLA_DK), ('c_v', GLA_WIDTH),
    ('c_lr_f', GLA_LOW_RANK), ('c_lr_b', GLA_LOW_RANK), ('c_gate', GLA_WIDTH),
)
N_IN = (3 * CONV_WIDTH + 2 * ATTN_WIDTH + 2 * ATTN_KV_HEADS * ATTN_HEAD_DIM
        + 2 * GLA_HEADS * GLA_DK + 2 * GLA_WIDTH + 2 * GLA_LOW_RANK)
CTX_KV_COLUMNS = ('b_k', 'b_v', 'c_k', 'c_v', 'c_lr_f', 'c_lr_b')

kernel_name = 'hybrid_conv_swa_gla_prefix_dit'


def _rms_norm(x):
    xf = x.astype(F32)
    return (xf * lax.rsqrt(jnp.mean(xf * xf, axis=-1, keepdims=True) + NORM_EPS)).astype(x.dtype)


def _column_slices():
    out, off = {}, 0
    for name, size in IN_COLUMNS:
        out[name] = (off, size)
        off += size
    return out


def _project(h, w_in, names=None):
    sl = _column_slices()
    if names is None:
        z = h @ w_in
        return {n: z[..., o:o + s] for n, (o, s) in sl.items()}
    return {n: h @ w_in[:, sl[n][0]:sl[n][0] + sl[n][1]] for n in names}


def _heads(t, n_heads):
    return t.reshape(t.shape[:2] + (n_heads, t.shape[-1] // n_heads))


def _rope_tables(n_tokens):
    rows = n_tokens // GRID_W
    r = jnp.repeat(jnp.arange(rows, dtype=F32), GRID_W)
    col = jnp.tile(jnp.arange(GRID_W, dtype=F32), rows)
    n_freq = ATTN_HEAD_DIM // 4
    inv = ROPE_THETA ** (-jnp.arange(n_freq, dtype=F32) / n_freq)
    ang = jnp.concatenate([r[:, None] * inv, col[:, None] * inv], axis=-1)
    ang = jnp.concatenate([ang, ang], axis=-1)
    return jnp.cos(ang), jnp.sin(ang)


def _apply_rope(x, cos, sin):
    xf = x.astype(F32)
    half = ATTN_HEAD_DIM // 2
    rot = jnp.concatenate([-xf[..., half:], xf[..., :half]], axis=-1)
    return (xf * cos[None, :, None, :] + rot * sin[None, :, None, :]).astype(x.dtype)


def _conv_branch(p, conv_w, conv_b, ln_w, ln_b):
    u = p['a_val'] * jax.nn.sigmoid(p['a_glu'])
    u = lax.conv_general_dilated(
        u, conv_w[:, None, :].astype(u.dtype), window_strides=(1,),
        padding=((CONV_KERNEL // 2, CONV_KERNEL // 2),),
        dimension_numbers=('NWC', 'WIO', 'NWC'), feature_group_count=CONV_WIDTH) + conv_b
    uf = u.astype(F32)
    mu = jnp.mean(uf, axis=-1, keepdims=True)
    var = jnp.mean(jnp.square(uf - mu), axis=-1, keepdims=True)
    u = ((uf - mu) * lax.rsqrt(var + NORM_EPS) * ln_w + ln_b).astype(p['a_val'].dtype)
    return jax.nn.silu(u) * jax.nn.silu(p['a_gate'])


def _windowed_attention(q, k, v, kc, vc, sink):
    B, S, H, Dh = q.shape
    nb = S // ATTN_BLOCK
    nc = kc.shape[1]
    qb = (q * Dh ** -0.5).reshape(B, nb, ATTN_BLOCK, ATTN_KV_HEADS, ATTN_GROUP, Dh).transpose(1, 0, 2, 3, 4, 5)

    def band(t):
        tb = t.reshape(B, nb, ATTN_BLOCK, ATTN_KV_HEADS, Dh)
        tp = jnp.pad(tb, ((0, 0), (1, 1), (0, 0), (0, 0), (0, 0)))
        return jnp.concatenate([tp[:, :-2], tp[:, 1:-1], tp[:, 2:]], axis=2).transpose(1, 0, 2, 3, 4)

    kw, vw = band(k), band(v)
    qi = jnp.arange(ATTN_BLOCK)
    kj = jnp.arange(3 * ATTN_BLOCK)
    rel = ATTN_BLOCK + qi[:, None] - kj[None, :]
    kpos = (jnp.arange(nb)[:, None] - 1) * ATTN_BLOCK + kj[None, :]
    mask = (jnp.abs(rel)[None] <= ATTN_WINDOW) & ((kpos >= 0) & (kpos < S))[:, None, :]
    sink_l = sink.astype(F32).reshape(ATTN_KV_HEADS, ATTN_GROUP)

    def one_block(args):
        qn, kn, vn, mn = args
        s_loc = jnp.einsum('bqhgd,bkhd->bhgqk', qn, kn).astype(F32)
        s_loc = jnp.where(mn, s_loc, NEG_INF)
        s_ctx = jnp.einsum('bqhgd,bchd->bhgqc', qn, kc).astype(F32)
        s_sink = jnp.broadcast_to(sink_l[None, :, :, None, None], s_ctx.shape[:-1] + (1,))
        p = jax.nn.softmax(jnp.concatenate([s_sink, s_ctx, s_loc], axis=-1), axis=-1)
        p_ctx = p[..., 1:1 + nc].astype(vn.dtype)
        p_loc = p[..., 1 + nc:].astype(vn.dtype)
        return (jnp.einsum('bhgqc,bchd->bqhgd', p_ctx, vc)
                + jnp.einsum('bhgqk,bkhd->bqhgd', p_loc, vn))

    o = lax.map(one_block, (qb, kw, vw, mask))
    return o.transpose(1, 0, 2, 3, 4, 5).reshape(B, S, H * Dh)


def _ctx_attention(qc, kc, vc, sink):
    B, Nc, H, Dh = qc.shape
    qg = (qc * Dh ** -0.5).reshape(B, Nc, ATTN_KV_HEADS, ATTN_GROUP, Dh)
    s = jnp.einsum('bqhgd,bkhd->bhgqk', qg, kc).astype(F32)
    s_sink = jnp.broadcast_to(sink.astype(F32).reshape(ATTN_KV_HEADS, ATTN_GROUP)[None, :, :, None, None],
                              s.shape[:-1] + (1,))
    p = jax.nn.softmax(jnp.concatenate([s_sink, s], axis=-1), axis=-1)[..., 1:].astype(vc.dtype)
    return jnp.einsum('bhgqk,bkhd->bqhgd', p, vc).reshape(B, Nc, H * Dh)


def _gla_decay(lr, w_up, b_up):
    z = lr.astype(F32) @ w_up.astype(F32) + b_up.astype(F32)
    return (jax.nn.log_sigmoid(z) / GLA_TAU).reshape(lr.shape[:2] + (GLA_HEADS, GLA_DK))


def _gla_chunked(q, k, v, log_a, s0):
    B, T, H, _ = q.shape
    dv = v.shape[-1]
    n = T // GLA_CHUNK

    def chunks(t):
        return t.astype(F32).reshape(B, n, GLA_CHUNK, H, t.shape[-1]).transpose(1, 0, 2, 3, 4)

    qc, kc, vc, la = chunks(q), chunks(k), chunks(v), chunks(log_a)
    b = jnp.cumsum(la, axis=2)
    b_last = b[:, :, -1:]
    q_in = qc * jnp.exp(b)
    k_in = kc * jnp.exp(-b)
    causal = jnp.tril(jnp.ones((GLA_CHUNK, GLA_CHUNK), dtype=bool))
    att = jnp.where(causal, jnp.einsum('nbthd,nbshd->nbhts', q_in, k_in), 0.0)
    o_intra = jnp.einsum('nbhts,nbshv->nbthv', att, vc)
    chunk_state = jnp.einsum('nbshd,nbshv->nbhdv', kc * jnp.exp(b_last - b), vc)
    chunk_decay = jnp.exp(b_last[:, :, 0])

    def step(s, xs):
        q_n, dec_n, st_n = xs
        o = jnp.einsum('bthd,bhdv->bthv', q_n, s)
        return dec_n[..., None] * s + st_n, o

    s_final, o_inter = lax.scan(step, s0.astype(F32), (q_in, chunk_decay, chunk_state))
    o = (o_intra + o_inter).transpose(1, 0, 2, 3, 4).reshape(B, T, H, dv)
    return o, s_final


def _gla_final_state(k, v, log_a):
    L = jnp.cumsum(log_a.astype(F32), axis=1)
    w = jnp.exp(L[:, -1:] - L)
    return jnp.einsum('bthd,bthv->bhdv', k.astype(F32) * w, v.astype(F32))


def _gla_output(o, gate, norm_w):
    B, T = o.shape[:2]
    of = o * lax.rsqrt(jnp.mean(o * o, axis=-1, keepdims=True) + NORM_EPS)
    of = of * norm_w.astype(F32).reshape(GLA_HEADS, GLA_DV)
    return of.reshape(B, T, GLA_WIDTH).astype(gate.dtype) * jax.nn.silu(gate)


def _flip(t):
    return jnp.flip(t, axis=1)


def _hybrid_layer(xl, xc, c, c_ctx, w_mod, b_mod, w_in, conv_w, conv_b, conv_ln_w, conv_ln_b,
                  attn_sink, gla_w_up, gla_b_up, gla_norm_w, w_out, rope_cos, rope_sin, update_ctx):
    B, S, _ = xl.shape
    D = D_MODEL
    shift, scale, gate = jnp.split(jax.nn.silu(c) @ w_mod + b_mod, 3, axis=-1)
    n_mod_c = 3 * D if update_ctx else 2 * D
    mod_c = jax.nn.silu(c_ctx) @ w_mod[:, :n_mod_c] + b_mod[:n_mod_c]
    hl = _rms_norm(xl) * (1.0 + scale[:, None]) + shift[:, None]
    hc = _rms_norm(xc) * (1.0 + mod_c[D:2 * D]) + mod_c[:D]
    pl = _project(hl, w_in)
    pc = _project(hc, w_in) if update_ctx else _project(hc, w_in, CTX_KV_COLUMNS)

    a_l = _conv_branch(pl, conv_w, conv_b, conv_ln_w, conv_ln_b)

    q_l = _apply_rope(_heads(pl['b_q'], ATTN_HEADS), rope_cos, rope_sin)
    k_l = _apply_rope(_heads(pl['b_k'], ATTN_KV_HEADS), rope_cos, rope_sin)
    v_l = _heads(pl['b_v'], ATTN_KV_HEADS)
    k_cx = _heads(pc['b_k'], ATTN_KV_HEADS)
    v_cx = _heads(pc['b_v'], ATTN_KV_HEADS)
    b_l = _windowed_attention(q_l, k_l, v_l, k_cx, v_cx, attn_sink) * jax.nn.silu(pl['b_gate'])

    gq_l = _heads(pl['c_q'], GLA_HEADS) * GLA_DK ** -0.5
    gk_l = _heads(pl['c_k'], GLA_HEADS)
    gv_l = _heads(pl['c_v'], GLA_HEADS)
    la_lf = _gla_decay(pl['c_lr_f'], gla_w_up[0], gla_b_up[0])
    la_lb = _gla_decay(pl['c_lr_b'], gla_w_up[1], gla_b_up[1])
    gk_c = _heads(pc['c_k'], GLA_HEADS)
    gv_c = _heads(pc['c_v'], GLA_HEADS)
    la_cf = _gla_decay(pc['c_lr_f'], gla_w_up[0], gla_b_up[0])
    la_cb = _gla_decay(pc['c_lr_b'], gla_w_up[1], gla_b_up[1])
    if update_ctx:
        gq_c = _heads(pc['c_q'], GLA_HEADS) * GLA_DK ** -0.5
        zeros = jnp.zeros((B, GLA_HEADS, GLA_DK, GLA_DV), F32)
        o_cf, s_f = _gla_chunked(gq_c, gk_c, gv_c, la_cf, zeros)
        o_cb, s_b = _gla_chunked(_flip(gq_c), _flip(gk_c), _flip(gv_c), _flip(la_cb), zeros)
    else:
        s_f = _gla_final_state(gk_c, gv_c, la_cf)
        s_b = _gla_final_state(_flip(gk_c), _flip(gv_c), _flip(la_cb))
    o_lf, _ = _gla_chunked(gq_l, gk_l, gv_l, la_lf, s_f)
    o_lb, _ = _gla_chunked(_flip(gq_l), _flip(gk_l), _flip(gv_l), _flip(la_lb), s_b)
    c_l = _gla_output(o_lf + _flip(o_lb), pl['c_gate'], gla_norm_w)

    y_l = jnp.concatenate([a_l, b_l, c_l], axis=-1) @ w_out
    xl = xl + gate[:, None] * y_l

    if update_ctx:
        a_c = _conv_branch(pc, conv_w, conv_b, conv_ln_w, conv_ln_b)
        q_cx = _heads(pc['b_q'], ATTN_HEADS)
        b_c = _ctx_attention(q_cx, k_cx, v_cx, attn_sink) * jax.nn.silu(pc['b_gate'])
        c_c = _gla_output(o_cf + _flip(o_cb), pc['c_gate'], gla_norm_w)
        y_c = jnp.concatenate([a_c, b_c, c_c], axis=-1) @ w_out
        xc = xc + mod_c[2 * D:] * y_c
    return xl, xc


def setup_inputs(seed: int = 0) -> dict:
    key = jax.random.key(seed)
    ks = jax.random.split(key, 17)
    nrm = jax.random.normal
    return {
        'x': nrm(ks[0], (BATCH, SEQ, D_MODEL), F32),
        'c': nrm(ks[1], (BATCH, D_MODEL), F32),
        'ctx': nrm(ks[2], (BATCH, CTX_LEN, D_MODEL), F32),
        'c_ctx': nrm(ks[3], (D_MODEL,), F32),
        'w_mod': nrm(ks[4], (DEPTH, D_MODEL, 3 * D_MODEL), F32) * (0.5 * D_MODEL ** -0.5),
        'b_mod': 0.01 * nrm(ks[5], (DEPTH, 3 * D_MODEL), F32),
        'w_in': nrm(ks[6], (DEPTH, D_MODEL, N_IN), F32) * D_MODEL ** -0.5,
        'conv_w': nrm(ks[7], (DEPTH, CONV_KERNEL, CONV_WIDTH), F32) * CONV_KERNEL ** -0.5,
        'conv_b': 0.01 * nrm(ks[8], (DEPTH, CONV_WIDTH), F32),
        'conv_ln_w': 1.0 + 0.01 * nrm(ks[9], (DEPTH, CONV_WIDTH), F32),
        'conv_ln_b': 0.01 * nrm(ks[10], (DEPTH, CONV_WIDTH), F32),
        'attn_sink': 0.5 * nrm(ks[11], (DEPTH, ATTN_HEADS), F32),
        'gla_w_up': nrm(ks[12], (DEPTH, 2, GLA_LOW_RANK, GLA_HEADS * GLA_DK), F32) * GLA_LOW_RANK ** -0.5,
        'gla_b_up': 0.1 * nrm(ks[13], (DEPTH, 2, GLA_HEADS * GLA_DK), F32),
        'gla_norm_w': 1.0 + 0.01 * nrm(ks[14], (DEPTH, GLA_WIDTH), F32),
        'w_out': nrm(ks[15], (DEPTH, D_MIX, D_MODEL), F32) * D_MIX ** -0.5,
        'final_norm_w': 1.0 + 0.01 * nrm(ks[16], (D_MODEL,), F32),
    }


def reference(x, c, ctx, c_ctx, w_mod, b_mod, w_in, conv_w, conv_b, conv_ln_w, conv_ln_b,
              attn_sink, gla_w_up, gla_b_up, gla_norm_w, w_out, final_norm_w):
    rope_cos, rope_sin = _rope_tables(x.shape[1])
    xl, xc = x, ctx
    for i in range(DEPTH):
        xl, xc = _hybrid_layer(
            xl, xc, c, c_ctx, w_mod[i], b_mod[i], w_in[i], conv_w[i], conv_b[i], conv_ln_w[i], conv_ln_b[i],
            attn_sink[i], gla_w_up[i], gla_b_up[i], gla_norm_w[i], w_out[i], rope_cos, rope_sin,
            update_ctx=(i < DEPTH - 1))
    return _rms_norm(xl) * final_norm_w
```

```python
import functools

import numpy as np
import jax
import jax.numpy as jnp
from jax import lax
from jax.experimental import pallas as pl
from jax.experimental.pallas import tpu as pltpu

F32 = jnp.float32
BF16 = jnp.bfloat16

D_MODEL = 1024
DEPTH = 2
GRID_W = 64
CONV_WIDTH = 256
CONV_KERNEL = 31
CONV_HALO = 16
ATTN_WIDTH = 512
ATTN_HEAD_DIM = 64
ATTN_HEADS = 8
ATTN_KV_HEADS = 2
ATTN_BLOCK = 128
ROPE_THETA = 10000.0
GLA_WIDTH = 256
GLA_HEADS = 4
GLA_DV = 64
GLA_DK = 32
GLA_LOW_RANK = 16
GLA_TAU = 16.0
GLA_CHUNK = 64
NORM_EPS = 1e-6
NEG_INF = -1e30
LANES = 128

_COLS = (('a_val', 256), ('a_glu', 256), ('a_gate', 256), ('b_q', 512), ('b_k', 128), ('b_v', 128),
         ('b_gate', 512), ('c_q', 128), ('c_k', 128), ('c_v', 256), ('c_lr_f', 16), ('c_lr_b', 16),
         ('c_gate', 256))
N_IN = sum(s for _, s in _COLS)
N_IN_PAD = 2944

P_A = 0
P_Q = 768
P_KV = 1280
P_BG = 1536
P_C = 2048
P_CG = 2560
P_LR = 2816


def _old_offsets():
    out, off = {}, 0
    for name, size in _COLS:
        out[name] = off
        off += size
    return out


def _split_head_tile(base, first_head):
    idx = []
    for j in range(LANES):
        h = first_head + (1 if (j % 64) >= 32 else 0)
        d = (j % 32) + (32 if j >= 64 else 0)
        idx.append(base + h * ATTN_HEAD_DIM + d)
    return idx


def _in_perm():
    o = _old_offsets()
    idx = list(range(o['a_val'], o['a_val'] + 768))
    for c in range(4):
        idx += _split_head_tile(o['b_q'], 2 * c)
    idx += _split_head_tile(o['b_k'], 0)
    idx += list(range(o['b_v'], o['b_v'] + 128))
    idx += list(range(o['b_gate'], o['b_gate'] + 512))
    idx += list(range(o['c_q'], o['c_q'] + 512))
    idx += list(range(o['c_gate'], o['c_gate'] + 256))
    idx += list(range(o['c_lr_f'], o['c_lr_f'] + 32))
    return np.asarray(idx, dtype=np.int32)


def _sigmoid(x):
    return 1.0 / (1.0 + jnp.exp(-x))


def _silu(x):
    return x * _sigmoid(x)


def _split_bf16(x):
    hi = x.astype(BF16)
    lo = (x - hi.astype(F32)).astype(BF16)
    return hi, lo


def _dot(a, b):
    return jnp.dot(a, b, preferred_element_type=F32)


def _dot_nt(a, b):
    return lax.dot_general(a, b, (((1,), (1,)), ((), ())), preferred_element_type=F32)


def _cparams(sem, vmem_mb):
    return pltpu.CompilerParams(dimension_semantics=sem, vmem_limit_bytes=vmem_mb * 1024 * 1024)


def _mod_kernel(cc_ref, w_ref, b_ref, o_ref):
    s = _silu(cc_ref[...])
    sh, sl = _split_bf16(s)
    wh, wl = _split_bf16(w_ref[0])
    o_ref[0] = _dot(sh, wh) + _dot(sl, wh) + _dot(sh, wl) + b_ref[0]


def _modulation(cc, w_mod, b_mod):
    nb = 3
    blk = 3 * D_MODEL // nb
    return pl.pallas_call(
        _mod_kernel,
        grid=(DEPTH, nb),
        in_specs=[pl.BlockSpec((16, D_MODEL), lambda l, j: (0, 0)),
                  pl.BlockSpec((1, D_MODEL, blk), lambda l, j: (l, 0, j)),
                  pl.BlockSpec((1, 1, blk), lambda l, j: (l, 0, j))],
        out_specs=pl.BlockSpec((1, 16, blk), lambda l, j: (l, 0, j)),
        out_shape=jax.ShapeDtypeStruct((DEPTH, 16, 3 * D_MODEL), F32),
        compiler_params=_cparams(("arbitrary", "arbitrary"), 32),
        name="modulation",
    )(cc, w_mod, b_mod.reshape(DEPTH, 1, 3 * D_MODEL))


def _proj_kernel(x_ref, mod_ref, cos_ref, sin_ref, w_ref, wuh_ref, wul_ref, bup_ref,
                 u_ref, q_ref, kv_ref, ga_ref, gb_ref, gc_ref, gqk_ref, gv_ref, la_ref):
    x = x_ref[0]
    mod = mod_ref[...]
    shift = mod[:, 0:D_MODEL]
    scale = mod[:, D_MODEL:2 * D_MODEL]
    ms = jnp.mean(x * x, axis=-1, keepdims=True)
    h = x * lax.rsqrt(ms + NORM_EPS) * (1.0 + scale) + shift
    hb = h.astype(BF16)

    def proj(a, b):
        return _dot(hb, w_ref[:, a:b])

    za = proj(P_A, P_A + 768)
    u_ref[0] = za[:, 0:256] * _sigmoid(za[:, 256:512])
    ga_ref[0] = _silu(za[:, 512:768])

    cos = cos_ref[...]
    sin = sin_ref[...]

    def rope(t):
        return t * cos + pltpu.roll(t, 64, 1) * sin

    zq = proj(P_Q, P_Q + 512)
    for c in range(4):
        r = rope(zq[:, c * LANES:(c + 1) * LANES]) * (ATTN_HEAD_DIM ** -0.5)
        q_ref[0, :, c * LANES:(c + 1) * LANES] = r.astype(BF16)

    zkv = proj(P_KV, P_KV + 256)
    lane = lax.broadcasted_iota(jnp.int32, (1, LANES), 1)
    even = (lane % 64) < 32
    low = lane < 64
    k = rope(zkv[:, 0:LANES])
    kv_ref[0, :, 0:128] = jnp.where(even, k, pltpu.roll(k, 32, 1)).astype(BF16)
    kv_ref[0, :, 128:256] = jnp.where(even, pltpu.roll(k, 96, 1), k).astype(BF16)
    v = zkv[:, LANES:2 * LANES]
    vr = pltpu.roll(v, 64, 1)
    kv_ref[0, :, 256:384] = jnp.where(low, v, vr).astype(BF16)
    kv_ref[0, :, 384:512] = jnp.where(low, vr, v).astype(BF16)

    gb_ref[0] = _silu(proj(P_BG, P_BG + 512))

    zc = proj(P_C, P_C + 512)
    gqk_ref[0, :, 0:128] = zc[:, 0:128] * (GLA_DK ** -0.5)
    gqk_ref[0, :, 128:256] = zc[:, 128:256]
    gv_ref[0] = zc[:, 256:512]
    gc_ref[0] = _silu(proj(P_CG, P_CG + 256))

    zlr = proj(P_LR, P_LR + LANES)
    zh, zl = _split_bf16(zlr)
    wuh = wuh_ref[...]
    zup = _dot(zh, wuh) + _dot(zl, wuh) + _dot(zh, wul_ref[...]) + bup_ref[...]
    la_ref[0] = (jnp.minimum(zup, 0.0) - jnp.log(1.0 + jnp.exp(-jnp.abs(zup)))) * (1.0 / GLA_TAU)


def _project(x, mod4, layer, mod_row, cos, sin, w_p, wuh, wul, bup, ts):
    B, S, _ = x.shape
    nt = S // ts
    if mod_row is None:
        mod_map = lambda b, t: (layer, b, 0, 0)
    else:
        mod_map = lambda b, t: (layer, mod_row, 0, 0)
    tok = lambda w: pl.BlockSpec((1, ts, w), lambda b, t: (b, t, 0))
    full = lambda a: pl.BlockSpec(a.shape, lambda b, t: (0,) * a.ndim)
    widths = (256, 512, 512, 256, 512, 256, 256, 256, 256)
    dtypes = (F32, BF16, BF16, F32, F32, F32, F32, F32, F32)
    return pl.pallas_call(
        _proj_kernel,
        grid=(B, nt),
        in_specs=[tok(D_MODEL),
                  pl.BlockSpec((None, None, 1, 3 * D_MODEL), mod_map),
                  pl.BlockSpec((ts, LANES), lambda b, t: (t, 0)),
                  pl.BlockSpec((ts, LANES), lambda b, t: (t, 0)),
                  full(w_p), full(wuh), full(wul), full(bup)],
        out_specs=[tok(w) for w in widths],
        out_shape=[jax.ShapeDtypeStruct((B, S, w), dt) for w, dt in zip(widths, dtypes)],
        compiler_params=_cparams(("parallel", "arbitrary"), 52),
        name="in_proj",
    )(x, mod4, cos, sin, w_p, wuh, wul, bup)


def _conv_kernel(um_ref, up_ref, un_ref, ga_ref, w_ref, b_ref, lnw_ref, lnb_ref, o_ref, buf_ref, *, nt, tc):
    t = pl.program_id(1)
    H = CONV_HALO
    buf_ref[0:H] = jnp.where(t > 0, up_ref[0], 0.0)
    buf_ref[H:H + tc] = um_ref[0]
    buf_ref[H + tc:2 * H + tc] = jnp.where(t < nt - 1, un_ref[0], 0.0)
    rows = 64
    off = H - CONV_KERNEL // 2
    for r in range(0, tc, rows):
        acc = jnp.zeros((rows, CONV_WIDTH), F32)
        for k in range(CONV_KERNEL):
            acc = acc + w_ref[k:k + 1, :] * buf_ref[r + off + k:r + off + k + rows, :]
        acc = acc + b_ref[...]
        mu = jnp.mean(acc, axis=-1, keepdims=True)
        cen = acc - mu
        var = jnp.mean(cen * cen, axis=-1, keepdims=True)
        y = cen * lax.rsqrt(var + NORM_EPS) * lnw_ref[...] + lnb_ref[...]
        o_ref[0, r:r + rows, :] = (_silu(y) * ga_ref[0, r:r + rows, :]).astype(BF16)


def _conv_branch(u, ga, conv_w, conv_b, ln_w, ln_b, tc):
    B, S, _ = u.shape
    nt = S // tc
    hb = tc // CONV_HALO
    nhb = S // CONV_HALO
    row = lambda a: a.reshape(1, CONV_WIDTH)
    vec = pl.BlockSpec((1, CONV_WIDTH), lambda b, t: (0, 0))
    return pl.pallas_call(
        functools.partial(_conv_kernel, nt=nt, tc=tc),
        grid=(B, nt),
        in_specs=[pl.BlockSpec((1, tc, CONV_WIDTH), lambda b, t: (b, t, 0)),
                  pl.BlockSpec((1, CONV_HALO, CONV_WIDTH), lambda b, t: (b, jnp.maximum(t * hb - 1, 0), 0)),
                  pl.BlockSpec((1, CONV_HALO, CONV_WIDTH),
                               lambda b, t: (b, jnp.minimum((t + 1) * hb, nhb - 1), 0)),
                  pl.BlockSpec((1, tc, CONV_WIDTH), lambda b, t: (b, t, 0)),
                  pl.BlockSpec((CONV_KERNEL, CONV_WIDTH), lambda b, t: (0, 0)),
                  vec, vec, vec],
        out_specs=pl.BlockSpec((1, tc, CONV_WIDTH), lambda b, t: (b, t, 0)),
        out_shape=jax.ShapeDtypeStruct((B, S, CONV_WIDTH), BF16),
        scratch_shapes=[pltpu.VMEM((tc + 2 * CONV_HALO, CONV_WIDTH), F32)],
        compiler_params=_cparams(("parallel", "arbitrary"), 32),
        name="conv_branch",
    )(u, u, u, ga, conv_w, row(conv_b), row(ln_w), row(ln_b))


def _attn_kernel(sink_ref, q_ref, kvm_ref, kvp_ref, kvn_ref, ckv_ref, gb_ref, o_ref, win_ref,
                 *, nsub, nblk, has_local):
    t = pl.program_id(1)
    A = ATTN_BLOCK
    lane = lax.broadcasted_iota(jnp.int32, (1, LANES), 1)
    m_even = jnp.where((lane % 64) < 32, 1.0, 0.0).astype(BF16)
    m_odd = jnp.where((lane % 64) < 32, 0.0, 1.0).astype(BF16)
    m_low = jnp.where(lane < 64, 1.0, 0.0).astype(BF16)
    m_high = jnp.where(lane < 64, 0.0, 1.0).astype(BF16)
    ck = ckv_ref[0]
    if has_local:
        win_ref[0:A] = kvp_ref[0]
        win_ref[A:A + nsub * A] = kvm_ref[0]
        win_ref[A + nsub * A:2 * A + nsub * A] = kvn_ref[0]
        qi = lax.broadcasted_iota(jnp.int32, (A, 3 * A), 0)
        cj = lax.broadcasted_iota(jnp.int32, (A, 3 * A), 1)
        band = (cj >= qi) & (cj <= qi + 2 * A)

    def body(j, carry):
        r0 = pl.multiple_of(j * A, A)
        if has_local:
            blk = t * nsub + j
            ok_prev = jnp.where(blk > 0, 0, A)
            ok_next = jnp.where(blk < nblk - 1, 3 * A, 2 * A)
            mask = band & (cj >= ok_prev) & (cj < ok_next)
        for g in range(ATTN_KV_HEADS):
            kc = ck[:, g * LANES:(g + 1) * LANES]
            vc = ck[:, 256 + g * LANES:256 + (g + 1) * LANES]
            if has_local:
                kw = win_ref[pl.ds(r0, 3 * A), g * LANES:(g + 1) * LANES]
                vw = win_ref[pl.ds(r0, 3 * A), 256 + g * LANES:256 + (g + 1) * LANES]
                vall = jnp.concatenate([vw, vc], axis=0)
            else:
                vall = vc
            v_low = vall * m_low
            v_high = vall * m_high
            for c in (2 * g, 2 * g + 1):
                qc = q_ref[0, pl.ds(r0, A), c * LANES:(c + 1) * LANES]
                acc = None
                for par, (qm, vm) in enumerate(((m_even, v_low), (m_odd, v_high))):
                    head = 2 * c + par
                    qh = qc * qm
                    sc = _dot_nt(qh, kc)
                    sink = sink_ref[head]
                    m = jnp.maximum(jnp.max(sc, axis=-1, keepdims=True), sink)
                    if has_local:
                        sl = jnp.where(mask, _dot_nt(qh, kw), NEG_INF)
                        m = jnp.maximum(m, jnp.max(sl, axis=-1, keepdims=True))
                        pl_ = jnp.exp(sl - m)
                    pc = jnp.exp(sc - m)
                    den = jnp.sum(pc, axis=-1, keepdims=True) + jnp.exp(sink - m)
                    if has_local:
                        den = den + jnp.sum(pl_, axis=-1, keepdims=True)
                        p = jnp.concatenate([pl_, pc], axis=1)
                    else:
                        p = pc
                    r = _dot(p.astype(BF16), vm) * (1.0 / den)
                    acc = r if acc is None else acc + r
                gate = gb_ref[0, pl.ds(r0, A), c * LANES:(c + 1) * LANES]
                o_ref[0, pl.ds(r0, A), c * LANES:(c + 1) * LANES] = (acc * gate).astype(BF16)
        return carry

    lax.fori_loop(0, nsub, body, 0)


def _attention(q, kv, ckv, gb, sink, tq, has_local):
    B, S, _ = q.shape
    nt = S // tq
    nsub = tq // ATTN_BLOCK
    nblk = S // ATTN_BLOCK
    A = ATTN_BLOCK
    kvw = kv.shape[-1]
    kernel = functools.partial(_attn_kernel, nsub=nsub, nblk=nblk, has_local=has_local)
    return pl.pallas_call(
        kernel,
        grid=(B, nt),
        in_specs=[pl.BlockSpec(memory_space=pltpu.SMEM),
                  pl.BlockSpec((1, tq, ATTN_WIDTH), lambda b, t: (b, t, 0)),
                  pl.BlockSpec((1, tq, kvw), lambda b, t: (b, t, 0)),
                  pl.BlockSpec((1, A, kvw), lambda b, t: (b, jnp.maximum(t * nsub - 1, 0), 0)),
                  pl.BlockSpec((1, A, kvw), lambda b, t: (b, jnp.minimum((t + 1) * nsub, nblk - 1), 0)),
                  pl.BlockSpec((1, ckv.shape[1], kvw), lambda b, t: (b, 0, 0)),
                  pl.BlockSpec((1, tq, ATTN_WIDTH), lambda b, t: (b, t, 0))],
        out_specs=pl.BlockSpec((1, tq, ATTN_WIDTH), lambda b, t: (b, t, 0)),
        scratch_shapes=[pltpu.VMEM((tq + 2 * A, kvw), BF16)],
        out_shape=jax.ShapeDtypeStruct((B, S, ATTN_WIDTH), BF16),
        compiler_params=_cparams(("parallel", "arbitrary"), 32),
        name="attention" if has_local else "ctx_attention",
    )(sink, q, kv, kv, kv, ckv, gb)


def _gla_kernel(*refs, reverse, tg, nt):
    if reverse:
        qk_ref, v_ref, la_ref, s0_ref, of_ref, gc_ref, nw_ref, o_ref, sfin_ref, st_ref = refs
    else:
        qk_ref, v_ref, la_ref, s0_ref, o_ref, sfin_ref, st_ref = refs
    t = pl.program_id(1)
    C = GLA_CHUNK

    @pl.when(t == 0)
    def _():
        st_ref[...] = s0_ref[0]

    r64 = lax.broadcasted_iota(jnp.int32, (C, C), 0)
    c64 = lax.broadcasted_iota(jnp.int32, (C, C), 1)
    tri = jnp.where((c64 >= r64) if reverse else (c64 <= r64), 1.0, 0.0).astype(BF16)
    lane_k = lax.broadcasted_iota(jnp.int32, (1, GLA_HEADS * GLA_DK), 1)
    lane_v = lax.broadcasted_iota(jnp.int32, (1, GLA_WIDTH), 1)
    hm = [jnp.where(lane_k // GLA_DK == h, 1.0, 0.0) for h in range(GLA_HEADS)]
    vm = [jnp.where(lane_v // GLA_DV == h, 1.0, 0.0) for h in range(GLA_HEADS)]
    rw = lax.broadcasted_iota(jnp.int32, (C, GLA_WIDTH), 0)
    sw = lax.broadcasted_iota(jnp.int32, (C, GLA_WIDTH), 1) % C
    cmask = (sw >= rw) if reverse else (sw <= rw)
    bd = (lax.broadcasted_iota(jnp.int32, (GLA_WIDTH, GLA_HEADS * GLA_DK), 0) // GLA_DV
          == lax.broadcasted_iota(jnp.int32, (GLA_WIDTH, GLA_HEADS * GLA_DK), 1) // GLA_DK)
    if reverse:
        hr = lax.broadcasted_iota(jnp.int32, (GLA_WIDTH, GLA_WIDTH), 0) // GLA_DV
        hc = lax.broadcasted_iota(jnp.int32, (GLA_WIDTH, GLA_WIDTH), 1) // GLA_DV
        head_mean = jnp.where(hr == hc, 1.0 / GLA_DV, 0.0).astype(BF16)

    nchunk = tg // C
    order = range(nchunk - 1, -1, -1) if reverse else range(nchunk)
    la_off = GLA_HEADS * GLA_DK if reverse else 0
    S = st_ref[...]
    for ci in order:
        r0 = ci * C
        q = qk_ref[0, r0:r0 + C, 0:128]
        k = qk_ref[0, r0:r0 + C, 128:256]
        v = v_ref[0, r0:r0 + C, :]
        la = la_ref[0, r0:r0 + C, la_off:la_off + 128]
        lah, lal = _split_bf16(la)
        b = _dot(tri, lah) + _dot(tri, lal)
        bt = b[0:1, :] if reverse else b[C - 1:C, :]
        q_in = (q * jnp.exp(b)).astype(BF16)
        k_in = k * jnp.exp(-b)
        k_st = (k * jnp.exp(bt - b)).astype(BF16)
        dec = jnp.exp(bt)
        kbd = jnp.concatenate([k_in * hm[h] for h in range(GLA_HEADS)], axis=0).astype(BF16)
        att = jnp.where(cmask, _dot_nt(q_in, kbd), 0.0).astype(BF16)
        vbd = jnp.concatenate([v * vm[h] for h in range(GLA_HEADS)], axis=0).astype(BF16)
        o = _dot(att, vbd) + _dot_nt(q_in, S.astype(BF16))
        cs = lax.dot_general(v.astype(BF16), k_st, (((0,), (0,)), ((), ())), preferred_element_type=F32)
        S = S * dec + jnp.where(bd, cs, 0.0)
        if reverse:
            ot = o + of_ref[0, r0:r0 + C, :]
            sh, sl = _split_bf16(ot * ot)
            msq = _dot(sh, head_mean) + _dot(sl, head_mean)
            y = ot * lax.rsqrt(msq + NORM_EPS) * nw_ref[...] * gc_ref[0, r0:r0 + C, :]
            o_ref[0, r0:r0 + C, :] = y.astype(BF16)
        else:
            o_ref[0, r0:r0 + C, :] = o
    st_ref[...] = S

    @pl.when(t == nt - 1)
    def _():
        sfin_ref[0] = S


def _gla_pass(qk, v, la, s0, reverse, tg, extra=None):
    B, S, _ = qk.shape
    nt = S // tg
    if reverse:
        tmap = lambda b, t: (b, nt - 1 - t, 0)
    else:
        tmap = lambda b, t: (b, t, 0)
    tok = pl.BlockSpec((1, tg, 256), tmap)
    st_spec = pl.BlockSpec((1, GLA_WIDTH, 128), lambda b, t: (b, 0, 0))
    in_specs = [tok, tok, tok, st_spec]
    args = [qk, v, la, s0]
    if reverse:
        of, gc, nw = extra
        in_specs += [tok, tok, pl.BlockSpec((1, GLA_WIDTH), lambda b, t: (0, 0))]
        args += [of, gc, nw.reshape(1, GLA_WIDTH)]
    return pl.pallas_call(
        functools.partial(_gla_kernel, reverse=reverse, tg=tg, nt=nt),
        grid=(B, nt),
        in_specs=in_specs,
        out_specs=[tok, st_spec],
        out_shape=[jax.ShapeDtypeStruct((B, S, GLA_WIDTH), BF16 if reverse else F32),
                   jax.ShapeDtypeStruct((B, GLA_WIDTH, 128), F32)],
        scratch_shapes=[pltpu.VMEM((GLA_WIDTH, 128), F32)],
        compiler_params=_cparams(("parallel", "arbitrary"), 32),
        name="gla_bwd" if reverse else "gla_fwd",
    )(*args)


def _gla_branch(gqk, gv, la, gc, norm_w, s0f, s0b, tg):
    of, sf = _gla_pass(gqk, gv, la, s0f, False, tg)
    c, sb = _gla_pass(gqk, gv, la, s0b, True, tg, extra=(of, gc, norm_w))
    return c, sf, sb


def _out_kernel(x_ref, a_ref, b_ref, c_ref, mod_ref, w_ref, fw_ref, o_ref, *, final):
    y = (_dot(a_ref[0], w_ref[0:256, :]) + _dot(b_ref[0], w_ref[256:768, :])
         + _dot(c_ref[0], w_ref[768:1024, :]))
    gate = mod_ref[:, 2 * D_MODEL:3 * D_MODEL]
    xn = x_ref[0] + gate * y
    if final:
        ms = jnp.mean(xn * xn, axis=-1, keepdims=True)
        xn = xn * lax.rsqrt(ms + NORM_EPS) * fw_ref[...]
    o_ref[0] = xn


def _out_proj(x, a, b, c, mod4, layer, mod_row, w_out, final_w, ts, final):
    B, S, _ = x.shape
    nt = S // ts
    if mod_row is None:
        mod_map = lambda bb, t: (layer, bb, 0, 0)
    else:
        mod_map = lambda bb, t: (layer, mod_row, 0, 0)
    tok = lambda w: pl.BlockSpec((1, ts, w), lambda bb, t: (bb, t, 0))
    return pl.pallas_call(
        functools.partial(_out_kernel, final=final),
        grid=(B, nt),
        in_specs=[tok(D_MODEL), tok(256), tok(512), tok(256),
                  pl.BlockSpec((None, None, 1, 3 * D_MODEL), mod_map),
                  pl.BlockSpec((D_MODEL, D_MODEL), lambda bb, t: (0, 0)),
                  pl.BlockSpec((1, D_MODEL), lambda bb, t: (0, 0))],
        out_specs=tok(D_MODEL),
        out_shape=jax.ShapeDtypeStruct((B, S, D_MODEL), F32),
        compiler_params=_cparams(("parallel", "arbitrary"), 40),
        name="out_proj",
    )(x, a, b, c, mod4, w_out, final_w.reshape(1, D_MODEL))


def _rope_tables(n_tokens):
    rows = n_tokens // GRID_W
    r = jnp.repeat(jnp.arange(rows, dtype=F32), GRID_W)
    col = jnp.tile(jnp.arange(GRID_W, dtype=F32), rows)
    n_freq = ATTN_HEAD_DIM // 4
    inv = ROPE_THETA ** (-jnp.arange(n_freq, dtype=F32) / n_freq)
    ang = jnp.concatenate([r[:, None] * inv, col[:, None] * inv], axis=-1)
    cos = jnp.tile(jnp.cos(ang), (1, 4))
    sin = jnp.sin(ang)
    return cos, jnp.concatenate([-sin, -sin, sin, sin], axis=-1)


def kernel(x, c, ctx, c_ctx, w_mod, b_mod, w_in, conv_w, conv_b, conv_ln_w, conv_ln_b, attn_sink,
           gla_w_up, gla_b_up, gla_norm_w, w_out, final_norm_w):
    B, S, D = x.shape
    NC = ctx.shape[1]
    TS, TC_CTX = 512, NC

    perm = _in_perm()
    w_p = jnp.concatenate([w_in[:, :, perm], jnp.zeros((DEPTH, D, N_IN_PAD - N_IN), F32)], axis=-1).astype(BF16)
    w_o = w_out.astype(BF16)
    w_up2 = jnp.zeros((DEPTH, LANES, 256), F32)
    w_up2 = w_up2.at[:, 0:GLA_LOW_RANK, 0:128].set(gla_w_up[:, 0])
    w_up2 = w_up2.at[:, GLA_LOW_RANK:2 * GLA_LOW_RANK, 128:256].set(gla_w_up[:, 1])
    wuh = w_up2.astype(BF16)
    wul = (w_up2 - wuh.astype(F32)).astype(BF16)
    bup = gla_b_up.reshape(DEPTH, 1, 256)
    cos_l, sin_l = _rope_tables(S)
    cos_c = jnp.ones((NC, LANES), F32)
    sin_c = jnp.zeros((NC, LANES), F32)
    zero_state = jnp.zeros((B, GLA_WIDTH, 128), F32)

    cc = jnp.concatenate([c, c_ctx[None, :], jnp.zeros((16 - B - 1, D), F32)], axis=0)
    mod4 = _modulation(cc, w_mod, b_mod).reshape(DEPTH, 16, 1, 3 * D)
    CTX_ROW = B

    xl, xc = x, ctx
    for i in range(DEPTH):
        last = i == DEPTH - 1
        (u_c, q_c, kv_c, ga_c, gb_c, gc_c, gqk_c, gv_c, la_c) = _project(
            xc, mod4, i, CTX_ROW, cos_c, sin_c, w_p[i], wuh[i], wul[i], bup[i], TC_CTX)
        c_c, s_f, s_b = _gla_branch(gqk_c, gv_c, la_c, gc_c, gla_norm_w[i], zero_state, zero_state, NC)
        if not last:
            a_c = _conv_branch(u_c, ga_c, conv_w[i], conv_b[i], conv_ln_w[i], conv_ln_b[i], NC)
            b_c = _attention(q_c, kv_c, kv_c, gb_c, attn_sink[i], NC, False)
            xc = _out_proj(xc, a_c, b_c, c_c, mod4, i, CTX_ROW, w_o[i], final_norm_w, NC, False)
        (u_l, q_l, kv_l, ga_l, gb_l, gc_l, gqk_l, gv_l, la_l) = _project(
            xl, mod4, i, None, cos_l, sin_l, w_p[i], wuh[i], wul[i], bup[i], TS)
        a_l = _conv_branch(u_l, ga_l, conv_w[i], conv_b[i], conv_ln_w[i], conv_ln_b[i], TS)
        b_l = _attention(q_l, kv_l, kv_c, gb_l, attn_sink[i], TS, True)
        c_l, _, _ = _gla_branch(gqk_l, gv_l, la_l, gc_l, gla_norm_w[i], s_f, s_b, TS)
        xl = _out_proj(xl, a_l, b_l, c_l, mod4, i, None, w_o[i], final_norm_w, TS, last)
    return xl
```

```python
import functools
import math

import numpy as np
import jax
import jax.numpy as jnp
from jax import lax
from jax.experimental import pallas as pl
from jax.experimental.pallas import tpu as pltpu

F32 = jnp.float32
BF16 = jnp.bfloat16

D_MODEL = 1024
DEPTH = 2
GRID_W = 64
CONV_WIDTH = 256
CONV_KERNEL = 31
CONV_HALO = 16
ATTN_WIDTH = 512
ATTN_HEAD_DIM = 64
ATTN_HEADS = 8
ATTN_KV_HEADS = 2
ATTN_BLOCK = 128
ROPE_THETA = 10000.0
GLA_WIDTH = 256
GLA_HEADS = 4
GLA_DV = 64
GLA_DK = 32
GLA_KW = GLA_HEADS * GLA_DK
GLA_LOW_RANK = 16
GLA_TAU = 16.0
GLA_CHUNK = 64
GLA_GROUP = 256
NORM_EPS = 1e-6
NEG_INF = -1e30
PASS_CAP = 3.0e38
LOG2E = math.log2(math.e)
LANES = 128
SUBLANES = 8

N_IN = 2848
N_IN_PAD = 2944

P_A = 0
P_Q = 768
P_KV = 1280
P_BG = 1536
P_C = 2048
P_CG = 2560
P_LR = 2816


def _sigmoid(x):
    return 1.0 / (1.0 + jnp.exp(-x))


def _silu(x):
    return x * _sigmoid(x)


def _split_bf16(x):
    hi = x.astype(BF16)
    lo = (x - hi.astype(F32)).astype(BF16)
    return hi, lo


def _dot(a, b):
    return jnp.dot(a, b, preferred_element_type=F32)


def _dot_nt(a, b):
    return lax.dot_general(a, b, (((1,), (1,)), ((), ())), preferred_element_type=F32)


def _dot_tn(a, b):
    return lax.dot_general(a, b, (((0,), (0,)), ((), ())), preferred_element_type=F32)


def _cparams(sem, vmem_mb):
    return pltpu.CompilerParams(dimension_semantics=sem, vmem_limit_bytes=vmem_mb * 1024 * 1024)


def _mod_kernel(cc_ref, w_ref, b_ref, o_ref):
    s = _silu(cc_ref[...])
    sh, sl = _split_bf16(s)
    wh, wl = _split_bf16(w_ref[0])
    o_ref[0] = _dot(sh, wh) + _dot(sl, wh) + _dot(sh, wl) + b_ref[0]


def _modulation(cc, w_mod, b_mod):
    nb = 3
    blk = 3 * D_MODEL // nb
    return pl.pallas_call(
        _mod_kernel,
        grid=(DEPTH, nb),
        in_specs=[pl.BlockSpec((16, D_MODEL), lambda l, j: (0, 0)),
                  pl.BlockSpec((1, D_MODEL, blk), lambda l, j: (l, 0, j)),
                  pl.BlockSpec((1, 1, blk), lambda l, j: (l, 0, j))],
        out_specs=pl.BlockSpec((1, 16, blk), lambda l, j: (l, 0, j)),
        out_shape=jax.ShapeDtypeStruct((DEPTH, 16, 3 * D_MODEL), F32),
        compiler_params=_cparams(("arbitrary", "arbitrary"), 32),
        name="modulation",
    )(cc, w_mod, b_mod.reshape(DEPTH, 1, 3 * D_MODEL))


def _proj_kernel(x_ref, mod_ref, cos_ref, sin_ref, w_ref, wuh_ref, wul_ref, bup_ref, tri_ref, sel_ref,
                 u_ref, q_ref, kk_ref, vt_ref, ga_ref, gb_ref, gc_ref, glf_ref, glb_ref, gv_ref, dec_ref):
    ts = x_ref.shape[1]
    x = x_ref[0]
    mod = mod_ref[...]
    shift = mod[:, 0:D_MODEL]
    scale = mod[:, D_MODEL:2 * D_MODEL]
    ms = jnp.mean(x * x, axis=-1, keepdims=True)
    h = x * lax.rsqrt(ms + NORM_EPS) * (1.0 + scale) + shift
    hb = h.astype(BF16)

    def proj(a, b):
        return _dot(hb, w_ref[:, a:b])

    za = proj(P_A, P_A + 768)
    u_ref[0] = za[:, 0:256] * _sigmoid(za[:, 256:512])
    ga_ref[0] = _silu(za[:, 512:768])

    cos = cos_ref[...]
    sin = sin_ref[...]

    def rope(t):
        return t * cos + pltpu.roll(t, 64, 1) * sin

    zq = proj(P_Q, P_Q + 512)
    for c in range(4):
        r = rope(zq[:, c * LANES:(c + 1) * LANES]) * (ATTN_HEAD_DIM ** -0.5 * LOG2E)
        q_ref[0, :, c * LANES:(c + 1) * LANES] = r.astype(BF16)

    zkv = proj(P_KV, P_KV + 256)
    lane = lax.broadcasted_iota(jnp.int32, (1, LANES), 1)
    even = (lane % 64) < 32
    k = rope(zkv[:, 0:LANES])
    kk_ref[0, :, 0:128] = jnp.where(even, k, pltpu.roll(k, 32, 1)).astype(BF16)
    kk_ref[0, :, 128:256] = jnp.where(even, pltpu.roll(k, 96, 1), k).astype(BF16)
    vt_ref[0] = zkv[:, LANES:2 * LANES].T.astype(BF16)

    gb_ref[0] = _silu(proj(P_BG, P_BG + 512))
    gc_ref[0] = _silu(proj(P_CG, P_CG + 256))

    zc = proj(P_C, P_C + 512)
    gq = zc[:, 0:128] * (GLA_DK ** -0.5)
    gk = zc[:, 128:256]
    gv_ref[0] = zc[:, 256:512].astype(BF16)

    zlr = proj(P_LR, P_LR + LANES)
    zh, zl = _split_bf16(zlr)
    wuh = wuh_ref[...]
    zup = _dot(zh, wuh) + _dot(zl, wuh) + _dot(zh, wul_ref[...]) + bup_ref[...]
    la = (jnp.minimum(zup, 0.0) - jnp.log(1.0 + jnp.exp(-jnp.abs(zup)))) * (1.0 / GLA_TAU)
    lah, lal = _split_bf16(la)

    nch = ts // GLA_CHUNK
    sel = sel_ref[...]
    for d, out_ref in enumerate((glf_ref, glb_ref)):
        tri = tri_ref[d]
        cols = slice(d * GLA_KW, (d + 1) * GLA_KW)
        b = jnp.concatenate(
            [_dot(tri, lah[r:r + GLA_GROUP, cols]) + _dot(tri, lal[r:r + GLA_GROUP, cols])
             for r in range(0, ts, GLA_GROUP)], axis=0)
        b3 = b.reshape(nch, GLA_CHUNK, GLA_KW)
        bt = b3[:, 0:1, :] if d else b3[:, GLA_CHUNK - 1:GLA_CHUNK, :]
        enb = jnp.exp(-b3)
        q_in = gq.reshape(nch, GLA_CHUNK, GLA_KW) * jnp.exp(b3)
        gk3 = gk.reshape(nch, GLA_CHUNK, GLA_KW)
        k_in = gk3 * enb
        k_st = gk3 * jnp.exp(bt - b3)
        out_ref[0, :, 0:128] = q_in.reshape(ts, GLA_KW).astype(BF16)
        out_ref[0, :, 128:256] = k_in.reshape(ts, GLA_KW).astype(BF16)
        out_ref[0, :, 256:384] = k_st.reshape(ts, GLA_KW).astype(BF16)
        tot = _dot(sel, lah[:, cols]) + _dot(sel, lal[:, cols])
        dec_ref[0, :, cols] = jnp.exp(tot)


def _project(x, mod4, layer, mod_row, cos, sin, w_p, wuh, wul, bup, tri, ts):
    B, S, _ = x.shape
    nt = S // ts
    nch = ts // GLA_CHUNK
    if mod_row is None:
        mod_map = lambda b, t: (layer, b, 0, 0)
    else:
        mod_map = lambda b, t: (layer, mod_row, 0, 0)
    sel = jnp.asarray(np.kron(np.eye(nch), np.ones((1, GLA_CHUNK))), BF16)
    tok = lambda w: pl.BlockSpec((1, ts, w), lambda b, t: (b, t, 0))
    full = lambda a: pl.BlockSpec(a.shape, lambda b, t: (0,) * a.ndim)
    tok_out = ((256, F32), (512, BF16), (256, BF16), None, (256, F32), (512, F32), (256, F32),
               (384, BF16), (384, BF16), (256, BF16), None)
    out_specs, out_shape = [], []
    for i, wd in enumerate(tok_out):
        if i == 3:
            out_specs.append(pl.BlockSpec((1, LANES, ts), lambda b, t: (b, 0, t)))
            out_shape.append(jax.ShapeDtypeStruct((B, LANES, S), BF16))
        elif i == 10:
            out_specs.append(pl.BlockSpec((1, nch, 2 * GLA_KW), lambda b, t: (b, t, 0)))
            out_shape.append(jax.ShapeDtypeStruct((B, S // GLA_CHUNK, 2 * GLA_KW), F32))
        else:
            out_specs.append(tok(wd[0]))
            out_shape.append(jax.ShapeDtypeStruct((B, S, wd[0]), wd[1]))
    return pl.pallas_call(
        _proj_kernel,
        grid=(B, nt),
        in_specs=[tok(D_MODEL),
                  pl.BlockSpec((None, None, 1, 3 * D_MODEL), mod_map),
                  pl.BlockSpec((ts, LANES), lambda b, t: (t, 0)),
                  pl.BlockSpec((ts, LANES), lambda b, t: (t, 0)),
                  full(w_p), full(wuh), full(wul), full(bup), full(tri), full(sel)],
        out_specs=out_specs,
        out_shape=out_shape,
        compiler_params=_cparams(("parallel", "arbitrary"), 56),
        name="in_proj",
    )(x, mod4, cos, sin, w_p, wuh, wul, bup, tri, sel)


def _conv_kernel(um_ref, up_ref, un_ref, ga_ref, w_ref, b_ref, lnw_ref, lnb_ref, o_ref, buf_ref, sh_ref,
                 *, nt, tc):
    t = pl.program_id(1)
    H = CONV_HALO
    buf_ref[0:H] = jnp.where(t > 0, up_ref[0], 0.0)
    buf_ref[H:H + tc] = um_ref[0]
    buf_ref[H + tc:2 * H + tc] = jnp.where(t < nt - 1, un_ref[0], 0.0)
    n_sh = sh_ref.shape[1]
    for p in range(SUBLANES):
        sh_ref[p] = buf_ref[p:p + n_sh, :]
    rows = 64
    off = H - CONV_KERNEL // 2
    for r in range(0, tc, rows):
        acc = jnp.zeros((rows, CONV_WIDTH), F32)
        for k in range(CONV_KERNEL):
            s = off + k
            a0 = r + (s // SUBLANES) * SUBLANES
            acc = acc + w_ref[k:k + 1, :] * sh_ref[s % SUBLANES, a0:a0 + rows, :]
        acc = acc + b_ref[...]
        mu = jnp.mean(acc, axis=-1, keepdims=True)
        cen = acc - mu
        var = jnp.mean(cen * cen, axis=-1, keepdims=True)
        y = cen * lax.rsqrt(var + NORM_EPS) * lnw_ref[...] + lnb_ref[...]
        o_ref[0, r:r + rows, :] = (_silu(y) * ga_ref[0, r:r + rows, :]).astype(BF16)


def _conv_branch(u, ga, conv_w, conv_b, ln_w, ln_b, tc):
    B, S, _ = u.shape
    nt = S // tc
    hb = tc // CONV_HALO
    nhb = S // CONV_HALO
    n_sh = tc + 2 * CONV_HALO - SUBLANES
    row = lambda a: a.reshape(1, CONV_WIDTH)
    vec = pl.BlockSpec((1, CONV_WIDTH), lambda b, t: (0, 0))
    return pl.pallas_call(
        functools.partial(_conv_kernel, nt=nt, tc=tc),
        grid=(B, nt),
        in_specs=[pl.BlockSpec((1, tc, CONV_WIDTH), lambda b, t: (b, t, 0)),
                  pl.BlockSpec((1, CONV_HALO, CONV_WIDTH), lambda b, t: (b, jnp.maximum(t * hb - 1, 0), 0)),
                  pl.BlockSpec((1, CONV_HALO, CONV_WIDTH),
                               lambda b, t: (b, jnp.minimum((t + 1) * hb, nhb - 1), 0)),
                  pl.BlockSpec((1, tc, CONV_WIDTH), lambda b, t: (b, t, 0)),
                  pl.BlockSpec((CONV_KERNEL, CONV_WIDTH), lambda b, t: (0, 0)),
                  vec, vec, vec],
        out_specs=pl.BlockSpec((1, tc, CONV_WIDTH), lambda b, t: (b, t, 0)),
        out_shape=jax.ShapeDtypeStruct((B, S, CONV_WIDTH), BF16),
        scratch_shapes=[pltpu.VMEM((tc + 2 * CONV_HALO, CONV_WIDTH), F32),
                        pltpu.VMEM((SUBLANES, n_sh, CONV_WIDTH), F32)],
        compiler_params=_cparams(("parallel", "arbitrary"), 32),
        name="conv_branch",
    )(u, u, u, ga, conv_w, row(conv_b), row(ln_w), row(ln_b))


def _attn_kernel(sink_ref, q_ref, km_ref, kp_ref, kn_ref, vm_ref, vp_ref, vn_ref, ck_ref, cv_ref, cap_ref,
                 gb_ref, o_ref, kwin_ref, vwin_ref, *, nsub, nblk, has_local):
    t = pl.program_id(1)
    A = ATTN_BLOCK
    G = ATTN_HEADS // ATTN_KV_HEADS
    lane = lax.broadcasted_iota(jnp.int32, (1, LANES), 1)
    m_even = jnp.where((lane % 64) < 32, 1.0, 0.0).astype(BF16)
    m_odd = jnp.where((lane % 64) < 32, 0.0, 1.0).astype(BF16)
    strip = lax.broadcasted_iota(jnp.int32, (1, G * A), 1) // A
    if has_local:
        kwin_ref[0:A] = kp_ref[0]
        kwin_ref[A:A + nsub * A] = km_ref[0]
        kwin_ref[A + nsub * A:2 * A + nsub * A] = kn_ref[0]
        vwin_ref[0] = vp_ref[0]
        for s in range(nsub):
            vwin_ref[1 + s] = vm_ref[0, :, s * A:(s + 1) * A]
        vwin_ref[nsub + 1] = vn_ref[0]

    def body(j, carry):
        r0 = pl.multiple_of(j * A, A)
        if has_local:
            blk = t * nsub + j
            cap = cap_ref[jnp.where(blk == 0, 1, 0) + jnp.where(blk == nblk - 1, 2, 0)]
            cap4 = jnp.concatenate([cap] * G, axis=1)
        for g in range(ATTN_KV_HEADS):
            q0 = q_ref[0, pl.ds(r0, A), (2 * g) * LANES:(2 * g + 1) * LANES]
            q1 = q_ref[0, pl.ds(r0, A), (2 * g + 1) * LANES:(2 * g + 2) * LANES]
            qs = jnp.concatenate([q0 * m_even, q0 * m_odd, q1 * m_even, q1 * m_odd], axis=0)
            sink = jnp.zeros((1, G * A), F32)
            for i in range(G):
                sink = jnp.where(strip == i, sink_ref[G * g + i] * LOG2E, sink)
            sc = _dot_nt(ck_ref[0, :, g * LANES:(g + 1) * LANES], qs)
            m = jnp.maximum(jnp.max(sc, axis=0, keepdims=True), sink)
            if has_local:
                sl = _dot_nt(kwin_ref[pl.ds(r0, 3 * A), g * LANES:(g + 1) * LANES], qs)
                sl = jnp.minimum(sl, cap4)
                m = jnp.maximum(m, jnp.max(sl, axis=0, keepdims=True))
            vc = cv_ref[0, g * 64:(g + 1) * 64, :]
            ones = jnp.ones((16, vc.shape[1]), BF16)
            ot = _dot(jnp.concatenate([vc, ones], axis=0), jnp.exp2(sc - m).astype(BF16))
            if has_local:
                vl = jnp.concatenate([vwin_ref[j + i, g * 64:(g + 1) * 64, :] for i in range(3)], axis=1)
                ones = jnp.ones((16, 3 * A), BF16)
                ot = ot + _dot(jnp.concatenate([vl, ones], axis=0), jnp.exp2(sl - m).astype(BF16))
            den = ot[64:65, :] + jnp.exp2(sink - m)
            o = ot[0:64, :] * (1.0 / den)
            for i in range(2):
                c = 2 * g + i
                pair = jnp.concatenate([o[:, (2 * i) * A:(2 * i + 1) * A], o[:, (2 * i + 1) * A:(2 * i + 2) * A]],
                                       axis=0)
                gate = gb_ref[0, pl.ds(r0, A), c * LANES:(c + 1) * LANES]
                o_ref[0, pl.ds(r0, A), c * LANES:(c + 1) * LANES] = (pair.T * gate).astype(BF16)
        return carry

    lax.fori_loop(0, nsub, body, 0)


def _attn_caps():
    A = ATTN_BLOCK
    c = np.arange(3 * A)[:, None]
    q = np.arange(A)[None, :]
    band = (c >= q) & (c <= q + 2 * A)
    caps = [band, band & (c >= A), band & (c < 2 * A)]
    return jnp.asarray(np.stack([np.where(b, PASS_CAP, NEG_INF) for b in caps]), F32)


def _attention(q, kk, vt, ckk, cvt, gb, sink, tq, has_local):
    B, S, _ = q.shape
    nt = S // tq
    nsub = tq // ATTN_BLOCK
    nblk = S // ATTN_BLOCK
    A = ATTN_BLOCK
    nc = ckk.shape[1]
    assert not has_local or nblk >= 2
    caps = _attn_caps()
    prev = lambda t: jnp.maximum(t * nsub - 1, 0)
    nxt = lambda t: jnp.minimum((t + 1) * nsub, nblk - 1)
    return pl.pallas_call(
        functools.partial(_attn_kernel, nsub=nsub, nblk=nblk, has_local=has_local),
        grid=(B, nt),
        in_specs=[pl.BlockSpec(memory_space=pltpu.SMEM),
                  pl.BlockSpec((1, tq, ATTN_WIDTH), lambda b, t: (b, t, 0)),
                  pl.BlockSpec((1, tq, 256), lambda b, t: (b, t, 0)),
                  pl.BlockSpec((1, A, 256), lambda b, t: (b, prev(t), 0)),
                  pl.BlockSpec((1, A, 256), lambda b, t: (b, nxt(t), 0)),
                  pl.BlockSpec((1, LANES, tq), lambda b, t: (b, 0, t)),
                  pl.BlockSpec((1, LANES, A), lambda b, t: (b, 0, prev(t))),
                  pl.BlockSpec((1, LANES, A), lambda b, t: (b, 0, nxt(t))),
                  pl.BlockSpec((1, nc, 256), lambda b, t: (b, 0, 0)),
                  pl.BlockSpec((1, LANES, nc), lambda b, t: (b, 0, 0)),
                  pl.BlockSpec(caps.shape, lambda b, t: (0, 0, 0)),
                  pl.BlockSpec((1, tq, ATTN_WIDTH), lambda b, t: (b, t, 0))],
        out_specs=pl.BlockSpec((1, tq, ATTN_WIDTH), lambda b, t: (b, t, 0)),
        out_shape=jax.ShapeDtypeStruct((B, S, ATTN_WIDTH), BF16),
        scratch_shapes=[pltpu.VMEM((tq + 2 * A, 256), BF16),
                        pltpu.VMEM((nsub + 2, LANES, A), BF16)],
        compiler_params=_cparams(("parallel", "arbitrary"), 32),
        name="attention" if has_local else "ctx_attention",
    )(sink, q, kk, kk, kk, vt, vt, vt, ckk, cvt, caps, gb)


def _gla_kernel(*refs, reverse, tg, nt):
    if reverse:
        gl_ref, v_ref, dec_ref, s0_ref, of_ref, gc_ref, nw_ref, o_ref, sfin_ref, st_ref = refs
    else:
        gl_ref, v_ref, dec_ref, s0_ref, o_ref, sfin_ref, st_ref = refs
    t = pl.program_id(1)
    C = GLA_CHUNK

    @pl.when(t == 0)
    def _():
        st_ref[...] = s0_ref[0]

    lane_k = lax.broadcasted_iota(jnp.int32, (1, GLA_KW), 1)
    lane_v = lax.broadcasted_iota(jnp.int32, (1, GLA_WIDTH), 1)
    hm = [jnp.where(lane_k // GLA_DK == h, 1.0, 0.0).astype(BF16) for h in range(GLA_HEADS)]
    vm = [jnp.where(lane_v // GLA_DV == h, 1.0, 0.0).astype(BF16) for h in range(GLA_HEADS)]
    rw = lax.broadcasted_iota(jnp.int32, (C, GLA_WIDTH), 0)
    sw = lax.broadcasted_iota(jnp.int32, (C, GLA_WIDTH), 1) % C
    cmask = (sw >= rw) if reverse else (sw <= rw)
    bd = (lax.broadcasted_iota(jnp.int32, (GLA_WIDTH, GLA_KW), 0) // GLA_DV
          == lax.broadcasted_iota(jnp.int32, (GLA_WIDTH, GLA_KW), 1) // GLA_DK)
    if reverse:
        hr = lax.broadcasted_iota(jnp.int32, (GLA_WIDTH, GLA_WIDTH), 0) // GLA_DV
        hc = lax.broadcasted_iota(jnp.int32, (GLA_WIDTH, GLA_WIDTH), 1) // GLA_DV
        head_mean = jnp.where(hr == hc, 1.0 / GLA_DV, 0.0).astype(BF16)

    nchunk = tg // C
    order = range(nchunk - 1, -1, -1) if reverse else range(nchunk)
    d_off = GLA_KW if reverse else 0
    S = st_ref[...]
    for ci in order:
        r0 = ci * C
        q_in = gl_ref[0, r0:r0 + C, 0:128]
        k_in = gl_ref[0, r0:r0 + C, 128:256]
        k_st = gl_ref[0, r0:r0 + C, 256:384]
        v = v_ref[0, r0:r0 + C, :]
        dec = dec_ref[0, ci:ci + 1, d_off:d_off + GLA_KW]
        kbd = jnp.concatenate([k_in * hm[h] for h in range(GLA_HEADS)], axis=0)
        att = jnp.where(cmask, _dot_nt(q_in, kbd), 0.0).astype(BF16)
        vbd = jnp.concatenate([v * vm[h] for h in range(GLA_HEADS)], axis=0)
        o = _dot(att, vbd) + _dot_nt(q_in, S.astype(BF16))
        S = S * dec + jnp.where(bd, _dot_tn(v, k_st), 0.0)
        if reverse:
            ot = o + of_ref[0, r0:r0 + C, :]
            sh, sl = _split_bf16(ot * ot)
            msq = _dot(sh, head_mean) + _dot(sl, head_mean)
            y = ot * lax.rsqrt(msq + NORM_EPS) * nw_ref[...] * gc_ref[0, r0:r0 + C, :]
            o_ref[0, r0:r0 + C, :] = y.astype(BF16)
        else:
            o_ref[0, r0:r0 + C, :] = o
    st_ref[...] = S

    @pl.when(t == nt - 1)
    def _():
        sfin_ref[0] = S


def _gla_pass(gl, v, dec, s0, reverse, tg, extra=None):
    B, S, _ = gl.shape
    nt = S // tg
    nch = tg // GLA_CHUNK
    if reverse:
        tmap = lambda b, t: (b, nt - 1 - t, 0)
    else:
        tmap = lambda b, t: (b, t, 0)
    tok = lambda w: pl.BlockSpec((1, tg, w), tmap)
    st_spec = pl.BlockSpec((1, GLA_WIDTH, GLA_KW), lambda b, t: (b, 0, 0))
    in_specs = [tok(384), tok(256), pl.BlockSpec((1, nch, 2 * GLA_KW), tmap), st_spec]
    args = [gl, v, dec, s0]
    if reverse:
        of, gc, nw = extra
        in_specs += [tok(256), tok(256), pl.BlockSpec((1, GLA_WIDTH), lambda b, t: (0, 0))]
        args += [of, gc, nw.reshape(1, GLA_WIDTH)]
    return pl.pallas_call(
        functools.partial(_gla_kernel, reverse=reverse, tg=tg, nt=nt),
        grid=(B, nt),
        in_specs=in_specs,
        out_specs=[tok(256), st_spec],
        out_shape=[jax.ShapeDtypeStruct((B, S, GLA_WIDTH), BF16 if reverse else F32),
                   jax.ShapeDtypeStruct((B, GLA_WIDTH, GLA_KW), F32)],
        scratch_shapes=[pltpu.VMEM((GLA_WIDTH, GLA_KW), F32)],
        compiler_params=_cparams(("parallel", "arbitrary"), 32),
        name="gla_bwd" if reverse else "gla_fwd",
    )(*args)


def _gla_branch(glf, glb, gv, dec, gc, norm_w, s0f, s0b, tg):
    of, sf = _gla_pass(glf, gv, dec, s0f, False, tg)
    c, sb = _gla_pass(glb, gv, dec, s0b, True, tg, extra=(of, gc, norm_w))
    return c, sf, sb


def _out_kernel(x_ref, a_ref, b_ref, c_ref, mod_ref, w_ref, fw_ref, o_ref, *, final):
    y = (_dot(a_ref[0], w_ref[0:256, :]) + _dot(b_ref[0], w_ref[256:768, :])
         + _dot(c_ref[0], w_ref[768:1024, :]))
    gate = mod_ref[:, 2 * D_MODEL:3 * D_MODEL]
    xn = x_ref[0] + gate * y
    if final:
        ms = jnp.mean(xn * xn, axis=-1, keepdims=True)
        xn = xn * lax.rsqrt(ms + NORM_EPS) * fw_ref[...]
    o_ref[0] = xn


def _out_proj(x, a, b, c, mod4, layer, mod_row, w_out, final_w, ts, final):
    B, S, _ = x.shape
    nt = S // ts
    if mod_row is None:
        mod_map = lambda bb, t: (layer, bb, 0, 0)
    else:
        mod_map = lambda bb, t: (layer, mod_row, 0, 0)
    tok = lambda w: pl.BlockSpec((1, ts, w), lambda bb, t: (bb, t, 0))
    return pl.pallas_call(
        functools.partial(_out_kernel, final=final),
        grid=(B, nt),
        in_specs=[tok(D_MODEL), tok(256), tok(512), tok(256),
                  pl.BlockSpec((None, None, 1, 3 * D_MODEL), mod_map),
                  pl.BlockSpec((D_MODEL, D_MODEL), lambda bb, t: (0, 0)),
                  pl.BlockSpec((1, D_MODEL), lambda bb, t: (0, 0))],
        out_specs=tok(D_MODEL),
        out_shape=jax.ShapeDtypeStruct((B, S, D_MODEL), F32),
        compiler_params=_cparams(("parallel", "arbitrary"), 40),
        name="out_proj",
    )(x, a, b, c, mod4, w_out, final_w.reshape(1, D_MODEL))


def _rope_tables(n_tokens):
    rows = n_tokens // GRID_W
    r = jnp.repeat(jnp.arange(rows, dtype=F32), GRID_W)
    col = jnp.tile(jnp.arange(GRID_W, dtype=F32), rows)
    n_freq = ATTN_HEAD_DIM // 4
    inv = ROPE_THETA ** (-jnp.arange(n_freq, dtype=F32) / n_freq)
    ang = jnp.concatenate([r[:, None] * inv, col[:, None] * inv], axis=-1)
    cos = jnp.tile(jnp.cos(ang), (1, 4))
    sin = jnp.sin(ang)
    return cos, jnp.concatenate([-sin, -sin, sin, sin], axis=-1)


def _split_heads(w, n_pairs):
    lead = w.shape[:-1]
    w = w.reshape(lead + (n_pairs, 2, 2, 32))
    return jnp.swapaxes(w, -2, -3).reshape(lead + (n_pairs * LANES,))


def _relayout_w_in(w_in):
    pad = jnp.zeros(w_in.shape[:-1] + (N_IN_PAD - N_IN,), F32)
    return jnp.concatenate([
        w_in[..., 0:768],
        _split_heads(w_in[..., 768:1280], 4),
        _split_heads(w_in[..., 1280:1408], 1),
        w_in[..., 1408:2560],
        w_in[..., 2592:2848],
        w_in[..., 2560:2592],
        pad], axis=-1).astype(BF16)


def _cumsum_matrices():
    n = GLA_GROUP // GLA_CHUNK
    low = np.kron(np.eye(n), np.tril(np.ones((GLA_CHUNK, GLA_CHUNK))))
    return jnp.asarray(np.stack([low, low.T]), BF16)


def kernel(x, c, ctx, c_ctx, w_mod, b_mod, w_in, conv_w, conv_b, conv_ln_w, conv_ln_b, attn_sink,
           gla_w_up, gla_b_up, gla_norm_w, w_out, final_norm_w):
    B, S, D = x.shape
    NC = ctx.shape[1]
    TS = 512

    w_p = _relayout_w_in(w_in)
    w_o = w_out.astype(BF16)
    w_up2 = jnp.zeros((DEPTH, LANES, 256), F32)
    w_up2 = w_up2.at[:, 0:GLA_LOW_RANK, 0:128].set(gla_w_up[:, 0])
    w_up2 = w_up2.at[:, GLA_LOW_RANK:2 * GLA_LOW_RANK, 128:256].set(gla_w_up[:, 1])
    wuh = w_up2.astype(BF16)
    wul = (w_up2 - wuh.astype(F32)).astype(BF16)
    bup = gla_b_up.reshape(DEPTH, 1, 256)
    tri = _cumsum_matrices()
    cos_l, sin_l = _rope_tables(S)
    cos_c = jnp.ones((NC, LANES), F32)
    sin_c = jnp.zeros((NC, LANES), F32)
    zero_state = jnp.zeros((B, GLA_WIDTH, GLA_KW), F32)

    cc = jnp.concatenate([c, c_ctx[None, :], jnp.zeros((16 - B - 1, D), F32)], axis=0)
    mod4 = _modulation(cc, w_mod, b_mod).reshape(DEPTH, 16, 1, 3 * D)
    CTX_ROW = B

    xl, xc = x, ctx
    for i in range(DEPTH):
        last = i == DEPTH - 1
        (u_c, q_c, kk_c, vt_c, ga_c, gb_c, gc_c, glf_c, glb_c, gv_c, dec_c) = _project(
            xc, mod4, i, CTX_ROW, cos_c, sin_c, w_p[i], wuh[i], wul[i], bup[i], tri, NC)
        c_c, s_f, s_b = _gla_branch(glf_c, glb_c, gv_c, dec_c, gc_c, gla_norm_w[i], zero_state, zero_state, NC)
        if not last:
            a_c = _conv_branch(u_c, ga_c, conv_w[i], conv_b[i], conv_ln_w[i], conv_ln_b[i], NC)
            b_c = _attention(q_c, kk_c, vt_c, kk_c, vt_c, gb_c, attn_sink[i], NC, False)
            xc = _out_proj(xc, a_c, b_c, c_c, mod4, i, CTX_ROW, w_o[i], final_norm_w, NC, False)
        (u_l, q_l, kk_l, vt_l, ga_l, gb_l, gc_l, glf_l, glb_l, gv_l, dec_l) = _project(
            xl, mod4, i, None, cos_l, sin_l, w_p[i], wuh[i], wul[i], bup[i], tri, TS)
        a_l = _conv_branch(u_l, ga_l, conv_w[i], conv_b[i], conv_ln_w[i], conv_ln_b[i], TS)
        b_l = _attention(q_l, kk_l, vt_l, kk_c, vt_c, gb_l, attn_sink[i], TS, True)
        c_l, _, _ = _gla_branch(glf_l, glb_l, gv_l, dec_l, gc_l, gla_norm_w[i], s_f, s_b, TS)
        xl = _out_proj(xl, a_l, b_l, c_l, mod4, i, None, w_o[i], final_norm_w, TS, last)
    return xl
```

```python
import functools
import math

import numpy as np
import jax
import jax.numpy as jnp
from jax import lax
from jax.experimental import pallas as pl
from jax.experimental.pallas import tpu as pltpu

F32 = jnp.float32
BF16 = jnp.bfloat16

D_MODEL = 1024
DEPTH = 2
GRID_W = 64
CONV_WIDTH = 256
CONV_KERNEL = 31
CONV_HALO = 16
ATTN_WIDTH = 512
ATTN_HEAD_DIM = 64
ATTN_HEADS = 8
ATTN_KV_HEADS = 2
ATTN_BLOCK = 128
ATTN_LOOKAHEAD = 2
ROPE_THETA = 10000.0
GLA_WIDTH = 256
GLA_HEADS = 4
GLA_DV = 64
GLA_DK = 32
GLA_KW = GLA_HEADS * GLA_DK
GLA_LOW_RANK = 16
GLA_TAU = 16.0
GLA_CHUNK = 64
GLA_LOOKAHEAD = 2
GLA_GROUP = 256
NORM_EPS = 1e-6
NEG_INF = -1e30
PASS_CAP = 3.0e38
LOG2E = math.log2(math.e)
LANES = 128
SUBLANES = 8

N_IN = 2848
N_IN_PAD = 2944

P_A = 0
P_Q = 768
P_KV = 1280
P_BG = 1536
P_C = 2048
P_CG = 2560
P_LR = 2816


def _sigmoid(x):
    return 1.0 / (1.0 + jnp.exp(-x))


def _silu(x):
    return x * _sigmoid(x)


def _split_bf16(x):
    hi = x.astype(BF16)
    lo = (x - hi.astype(F32)).astype(BF16)
    return hi, lo


def _dot(a, b):
    return jnp.dot(a, b, preferred_element_type=F32)


def _dot_nt(a, b):
    return lax.dot_general(a, b, (((1,), (1,)), ((), ())), preferred_element_type=F32)


def _dot_tn(a, b):
    return lax.dot_general(a, b, (((0,), (0,)), ((), ())), preferred_element_type=F32)


def _cparams(sem, vmem_mb):
    return pltpu.CompilerParams(dimension_semantics=sem, vmem_limit_bytes=vmem_mb * 1024 * 1024)


def _mod_kernel(cc_ref, w_ref, b_ref, o_ref):
    s = _silu(cc_ref[...])
    sh, sl = _split_bf16(s)
    wh, wl = _split_bf16(w_ref[0])
    o_ref[0] = _dot(sh, wh) + _dot(sl, wh) + _dot(sh, wl) + b_ref[0]


def _modulation(cc, w_mod, b_mod):
    nb = 3
    blk = 3 * D_MODEL // nb
    return pl.pallas_call(
        _mod_kernel,
        grid=(DEPTH, nb),
        in_specs=[pl.BlockSpec((16, D_MODEL), lambda l, j: (0, 0)),
                  pl.BlockSpec((1, D_MODEL, blk), lambda l, j: (l, 0, j)),
                  pl.BlockSpec((1, 1, blk), lambda l, j: (l, 0, j))],
        out_specs=pl.BlockSpec((1, 16, blk), lambda l, j: (l, 0, j)),
        out_shape=jax.ShapeDtypeStruct((DEPTH, 16, 3 * D_MODEL), F32),
        compiler_params=_cparams(("arbitrary", "arbitrary"), 32),
        name="modulation",
    )(cc, w_mod, b_mod.reshape(DEPTH, 1, 3 * D_MODEL))


def _proj_kernel(x_ref, mod_ref, cos_ref, sin_ref, w_ref, wuh_ref, wul_ref, bup_ref, tri_ref, sel_ref,
                 u_ref, q_ref, kk_ref, vt_ref, ga_ref, gb_ref, gc_ref, glf_ref, glb_ref, gv_ref, dec_ref):
    ts = x_ref.shape[1]
    x = x_ref[0]
    mod = mod_ref[...]
    shift = mod[:, 0:D_MODEL]
    scale = mod[:, D_MODEL:2 * D_MODEL]
    ms = jnp.mean(x * x, axis=-1, keepdims=True)
    h = x * lax.rsqrt(ms + NORM_EPS) * (1.0 + scale) + shift
    hb = h.astype(BF16)

    def proj(a, b):
        return _dot(hb, w_ref[:, a:b])

    zlr = proj(P_LR, P_LR + LANES)
    zh, zl = _split_bf16(zlr)

    zc = proj(P_C, P_C + 512)
    gq = zc[:, 0:128] * (GLA_DK ** -0.5)
    gk = zc[:, 128:256]
    gv_ref[0] = zc[:, 256:512].astype(BF16)

    za = proj(P_A, P_A + 768)
    u_ref[0] = za[:, 0:256] * _sigmoid(za[:, 256:512])
    ga_ref[0] = _silu(za[:, 512:768])

    wuh = wuh_ref[...]
    zup = _dot(zh, wuh) + _dot(zl, wuh) + _dot(zh, wul_ref[...]) + bup_ref[...]
    la = (jnp.minimum(zup, 0.0) - jnp.log(1.0 + jnp.exp(-jnp.abs(zup)))) * (1.0 / GLA_TAU)
    lah, lal = _split_bf16(la)

    cos = cos_ref[...]
    sin = sin_ref[...]

    def rope(t):
        return t * cos + pltpu.roll(t, 64, 1) * sin

    zq = proj(P_Q, P_Q + 512)
    for c in range(4):
        r = rope(zq[:, c * LANES:(c + 1) * LANES]) * (ATTN_HEAD_DIM ** -0.5 * LOG2E)
        q_ref[0, :, c * LANES:(c + 1) * LANES] = r.astype(BF16)

    zkv = proj(P_KV, P_KV + 256)
    lane = lax.broadcasted_iota(jnp.int32, (1, LANES), 1)
    even = (lane % 64) < 32
    k = rope(zkv[:, 0:LANES])
    kk_ref[0, :, 0:128] = jnp.where(even, k, pltpu.roll(k, 32, 1)).astype(BF16)
    kk_ref[0, :, 128:256] = jnp.where(even, pltpu.roll(k, 96, 1), k).astype(BF16)
    vt_ref[0] = zkv[:, LANES:2 * LANES].T.astype(BF16)

    nch = ts // GLA_CHUNK
    sel = sel_ref[...]
    for d, out_ref in enumerate((glf_ref, glb_ref)):
        tri = tri_ref[d]
        cols = slice(d * GLA_KW, (d + 1) * GLA_KW)
        b = jnp.concatenate(
            [_dot(tri, lah[r:r + GLA_GROUP, cols]) + _dot(tri, lal[r:r + GLA_GROUP, cols])
             for r in range(0, ts, GLA_GROUP)], axis=0)
        tot = _dot(sel, lah[:, cols]) + _dot(sel, lal[:, cols])
        b3 = b.reshape(nch, GLA_CHUNK, GLA_KW)
        bt = b3[:, 0:1, :] if d else b3[:, GLA_CHUNK - 1:GLA_CHUNK, :]
        enb = jnp.exp(-b3)
        q_in = gq.reshape(nch, GLA_CHUNK, GLA_KW) * jnp.exp(b3)
        gk3 = gk.reshape(nch, GLA_CHUNK, GLA_KW)
        k_in = gk3 * enb
        k_st = gk3 * jnp.exp(bt - b3)
        out_ref[0, :, 0:128] = q_in.reshape(ts, GLA_KW).astype(BF16)
        out_ref[0, :, 128:256] = k_in.reshape(ts, GLA_KW).astype(BF16)
        out_ref[0, :, 256:384] = k_st.reshape(ts, GLA_KW).astype(BF16)
        dec_ref[0, :, cols] = jnp.exp(tot)

    gb_ref[0] = _silu(proj(P_BG, P_BG + 512))
    gc_ref[0] = _silu(proj(P_CG, P_CG + 256))


def _project(x, mod4, layer, mod_row, cos, sin, w_p, wuh, wul, bup, tri, ts):
    B, S, _ = x.shape
    nt = S // ts
    nch = ts // GLA_CHUNK
    if mod_row is None:
        mod_map = lambda b, t: (layer, b, 0, 0)
    else:
        mod_map = lambda b, t: (layer, mod_row, 0, 0)
    sel = jnp.asarray(np.kron(np.eye(nch), np.ones((1, GLA_CHUNK))), BF16)
    tok = lambda w: pl.BlockSpec((1, ts, w), lambda b, t: (b, t, 0))
    full = lambda a: pl.BlockSpec(a.shape, lambda b, t: (0,) * a.ndim)
    tok_out = ((256, F32), (512, BF16), (256, BF16), None, (256, F32), (512, F32), (256, F32),
               (384, BF16), (384, BF16), (256, BF16), None)
    out_specs, out_shape = [], []
    for i, wd in enumerate(tok_out):
        if i == 3:
            out_specs.append(pl.BlockSpec((1, LANES, ts), lambda b, t: (b, 0, t)))
            out_shape.append(jax.ShapeDtypeStruct((B, LANES, S), BF16))
        elif i == 10:
            out_specs.append(pl.BlockSpec((1, nch, 2 * GLA_KW), lambda b, t: (b, t, 0)))
            out_shape.append(jax.ShapeDtypeStruct((B, S // GLA_CHUNK, 2 * GLA_KW), F32))
        else:
            out_specs.append(tok(wd[0]))
            out_shape.append(jax.ShapeDtypeStruct((B, S, wd[0]), wd[1]))
    return pl.pallas_call(
        _proj_kernel,
        grid=(B, nt),
        in_specs=[tok(D_MODEL),
                  pl.BlockSpec((None, None, 1, 3 * D_MODEL), mod_map),
                  pl.BlockSpec((ts, LANES), lambda b, t: (t, 0)),
                  pl.BlockSpec((ts, LANES), lambda b, t: (t, 0)),
                  full(w_p), full(wuh), full(wul), full(bup), full(tri), full(sel)],
        out_specs=out_specs,
        out_shape=out_shape,
        compiler_params=_cparams(("parallel", "arbitrary"), 56),
        name="in_proj",
    )(x, mod4, cos, sin, w_p, wuh, wul, bup, tri, sel)


def _conv_kernel(um_ref, up_ref, un_ref, ga_ref, w_ref, b_ref, lnw_ref, lnb_ref, o_ref, buf_ref, sh_ref,
                 *, nt, tc):
    t = pl.program_id(1)
    H = CONV_HALO
    buf_ref[0:H] = jnp.where(t > 0, up_ref[0], 0.0)
    buf_ref[H:H + tc] = um_ref[0]
    buf_ref[H + tc:2 * H + tc] = jnp.where(t < nt - 1, un_ref[0], 0.0)
    n_sh = sh_ref.shape[1]
    for p in range(SUBLANES):
        sh_ref[p] = buf_ref[p:p + n_sh, :]
    rows = 64
    off = H - CONV_KERNEL // 2
    for r in range(0, tc, rows):
        acc = jnp.zeros((rows, CONV_WIDTH), F32)
        for k in range(CONV_KERNEL):
            s = off + k
            a0 = r + (s // SUBLANES) * SUBLANES
            acc = acc + w_ref[k:k + 1, :] * sh_ref[s % SUBLANES, a0:a0 + rows, :]
        acc = acc + b_ref[...]
        mu = jnp.mean(acc, axis=-1, keepdims=True)
        cen = acc - mu
        var = jnp.mean(cen * cen, axis=-1, keepdims=True)
        y = cen * lax.rsqrt(var + NORM_EPS) * lnw_ref[...] + lnb_ref[...]
        o_ref[0, r:r + rows, :] = (_silu(y) * ga_ref[0, r:r + rows, :]).astype(BF16)


def _conv_branch(u, ga, conv_w, conv_b, ln_w, ln_b, tc):
    B, S, _ = u.shape
    nt = S // tc
    hb = tc // CONV_HALO
    nhb = S // CONV_HALO
    n_sh = tc + 2 * CONV_HALO - SUBLANES
    row = lambda a: a.reshape(1, CONV_WIDTH)
    vec = pl.BlockSpec((1, CONV_WIDTH), lambda b, t: (0, 0))
    return pl.pallas_call(
        functools.partial(_conv_kernel, nt=nt, tc=tc),
        grid=(B, nt),
        in_specs=[pl.BlockSpec((1, tc, CONV_WIDTH), lambda b, t: (b, t, 0)),
                  pl.BlockSpec((1, CONV_HALO, CONV_WIDTH), lambda b, t: (b, jnp.maximum(t * hb - 1, 0), 0)),
                  pl.BlockSpec((1, CONV_HALO, CONV_WIDTH),
                               lambda b, t: (b, jnp.minimum((t + 1) * hb, nhb - 1), 0)),
                  pl.BlockSpec((1, tc, CONV_WIDTH), lambda b, t: (b, t, 0)),
                  pl.BlockSpec((CONV_KERNEL, CONV_WIDTH), lambda b, t: (0, 0)),
                  vec, vec, vec],
        out_specs=pl.BlockSpec((1, tc, CONV_WIDTH), lambda b, t: (b, t, 0)),
        out_shape=jax.ShapeDtypeStruct((B, S, CONV_WIDTH), BF16),
        scratch_shapes=[pltpu.VMEM((tc + 2 * CONV_HALO, CONV_WIDTH), F32),
                        pltpu.VMEM((SUBLANES, n_sh, CONV_WIDTH), F32)],
        compiler_params=_cparams(("parallel", "arbitrary"), 32),
        name="conv_branch",
    )(u, u, u, ga, conv_w, row(conv_b), row(ln_w), row(ln_b))


def _attn_kernel(sink_ref, q_ref, km_ref, kp_ref, kn_ref, vm_ref, vp_ref, vn_ref, ck_ref, cv_ref, cap_ref,
                 gb_ref, o_ref, kwin_ref, vwin_ref, *, nsub, nblk, has_local):
    t = pl.program_id(1)
    A = ATTN_BLOCK
    G = ATTN_HEADS // ATTN_KV_HEADS
    lane = lax.broadcasted_iota(jnp.int32, (1, LANES), 1)
    m_even = jnp.where((lane % 64) < 32, 1.0, 0.0).astype(BF16)
    m_odd = jnp.where((lane % 64) < 32, 0.0, 1.0).astype(BF16)
    strip = lax.broadcasted_iota(jnp.int32, (1, G * A), 1) // A
    if has_local:
        kwin_ref[0:A] = kp_ref[0]
        kwin_ref[A:A + nsub * A] = km_ref[0]
        kwin_ref[A + nsub * A:2 * A + nsub * A] = kn_ref[0]
        vwin_ref[0] = vp_ref[0]
        for s in range(nsub):
            vwin_ref[1 + s] = vm_ref[0, :, s * A:(s + 1) * A]
        vwin_ref[nsub + 1] = vn_ref[0]

    def scores(j, g):
        r0 = j * A
        q0 = q_ref[0, r0:r0 + A, (2 * g) * LANES:(2 * g + 1) * LANES]
        q1 = q_ref[0, r0:r0 + A, (2 * g + 1) * LANES:(2 * g + 2) * LANES]
        qs = jnp.concatenate([q0 * m_even, q0 * m_odd, q1 * m_even, q1 * m_odd], axis=0)
        sc = _dot_nt(ck_ref[0, :, g * LANES:(g + 1) * LANES], qs)
        sl = None
        if has_local:
            sl = _dot_nt(kwin_ref[r0:r0 + 3 * A, g * LANES:(g + 1) * LANES], qs)
        return sc, sl

    def finish(j, g, sc, sl):
        r0 = j * A
        sink = jnp.zeros((1, G * A), F32)
        for i in range(G):
            sink = jnp.where(strip == i, sink_ref[G * g + i] * LOG2E, sink)
        m = jnp.maximum(jnp.max(sc, axis=0, keepdims=True), sink)
        if has_local:
            blk = t * nsub + j
            cap = cap_ref[jnp.where(blk == 0, 1, 0) + jnp.where(blk == nblk - 1, 2, 0)]
            sl = jnp.minimum(sl, jnp.concatenate([cap] * G, axis=1))
            m = jnp.maximum(m, jnp.max(sl, axis=0, keepdims=True))
        vc = cv_ref[0, g * 64:(g + 1) * 64, :]
        ones = jnp.ones((16, vc.shape[1]), BF16)
        ot = _dot(jnp.concatenate([vc, ones], axis=0), jnp.exp2(sc - m).astype(BF16))
        if has_local:
            vl = jnp.concatenate([vwin_ref[j + i, g * 64:(g + 1) * 64, :] for i in range(3)], axis=1)
            ones = jnp.ones((16, 3 * A), BF16)
            ot = ot + _dot(jnp.concatenate([vl, ones], axis=0), jnp.exp2(sl - m).astype(BF16))
        den = ot[64:65, :] + jnp.exp2(sink - m)
        o = ot[0:64, :] * (1.0 / den)
        for i in range(2):
            c = 2 * g + i
            pair = jnp.concatenate([o[:, (2 * i) * A:(2 * i + 1) * A], o[:, (2 * i + 1) * A:(2 * i + 2) * A]],
                                   axis=0)
            gate = gb_ref[0, r0:r0 + A, c * LANES:(c + 1) * LANES]
            o_ref[0, r0:r0 + A, c * LANES:(c + 1) * LANES] = (pair.T * gate).astype(BF16)

    units = [(j, g) for j in range(nsub) for g in range(ATTN_KV_HEADS)]
    ahead = min(ATTN_LOOKAHEAD, len(units))
    pending = [scores(*u) for u in units[:ahead]]
    for n, unit in enumerate(units):
        if n + ahead < len(units):
            pending.append(scores(*units[n + ahead]))
        finish(*unit, *pending.pop(0))


def _attn_caps():
    A = ATTN_BLOCK
    c = np.arange(3 * A)[:, None]
    q = np.arange(A)[None, :]
    band = (c >= q) & (c <= q + 2 * A)
    caps = [band, band & (c >= A), band & (c < 2 * A)]
    return jnp.asarray(np.stack([np.where(b, PASS_CAP, NEG_INF) for b in caps]), F32)


def _attention(q, kk, vt, ckk, cvt, gb, sink, tq, has_local):
    B, S, _ = q.shape
    nt = S // tq
    nsub = tq // ATTN_BLOCK
    nblk = S // ATTN_BLOCK
    A = ATTN_BLOCK
    nc = ckk.shape[1]
    assert not has_local or nblk >= 2
    caps = _attn_caps()
    prev = lambda t: jnp.maximum(t * nsub - 1, 0)
    nxt = lambda t: jnp.minimum((t + 1) * nsub, nblk - 1)
    return pl.pallas_call(
        functools.partial(_attn_kernel, nsub=nsub, nblk=nblk, has_local=has_local),
        grid=(B, nt),
        in_specs=[pl.BlockSpec(memory_space=pltpu.SMEM),
                  pl.BlockSpec((1, tq, ATTN_WIDTH), lambda b, t: (b, t, 0)),
                  pl.BlockSpec((1, tq, 256), lambda b, t: (b, t, 0)),
                  pl.BlockSpec((1, A, 256), lambda b, t: (b, prev(t), 0)),
                  pl.BlockSpec((1, A, 256), lambda b, t: (b, nxt(t), 0)),
                  pl.BlockSpec((1, LANES, tq), lambda b, t: (b, 0, t)),
                  pl.BlockSpec((1, LANES, A), lambda b, t: (b, 0, prev(t))),
                  pl.BlockSpec((1, LANES, A), lambda b, t: (b, 0, nxt(t))),
                  pl.BlockSpec((1, nc, 256), lambda b, t: (b, 0, 0)),
                  pl.BlockSpec((1, LANES, nc), lambda b, t: (b, 0, 0)),
                  pl.BlockSpec(caps.shape, lambda b, t: (0, 0, 0)),
                  pl.BlockSpec((1, tq, ATTN_WIDTH), lambda b, t: (b, t, 0))],
        out_specs=pl.BlockSpec((1, tq, ATTN_WIDTH), lambda b, t: (b, t, 0)),
        out_shape=jax.ShapeDtypeStruct((B, S, ATTN_WIDTH), BF16),
        scratch_shapes=[pltpu.VMEM((tq + 2 * A, 256), BF16),
                        pltpu.VMEM((nsub + 2, LANES, A), BF16)],
        compiler_params=_cparams(("parallel", "arbitrary"), 32),
        name="attention" if has_local else "ctx_attention",
    )(sink, q, kk, kk, kk, vt, vt, vt, ckk, cvt, caps, gb)


def _gla_kernel(*refs, reverse, tg, nt):
    if reverse:
        gl_ref, v_ref, dec_ref, s0_ref, of_ref, gc_ref, nw_ref, o_ref, sfin_ref, st_ref = refs
    else:
        gl_ref, v_ref, dec_ref, s0_ref, o_ref, sfin_ref, st_ref = refs
    t = pl.program_id(1)
    C = GLA_CHUNK

    @pl.when(t == 0)
    def _():
        st_ref[...] = s0_ref[0]

    lane_k = lax.broadcasted_iota(jnp.int32, (1, GLA_KW), 1)
    lane_v = lax.broadcasted_iota(jnp.int32, (1, GLA_WIDTH), 1)
    hm = [jnp.where(lane_k // GLA_DK == h, 1.0, 0.0).astype(BF16) for h in range(GLA_HEADS)]
    vm = [jnp.where(lane_v // GLA_DV == h, 1.0, 0.0).astype(BF16) for h in range(GLA_HEADS)]
    rw = lax.broadcasted_iota(jnp.int32, (C, GLA_WIDTH), 0)
    sw = lax.broadcasted_iota(jnp.int32, (C, GLA_WIDTH), 1) % C
    cmask = (sw >= rw) if reverse else (sw <= rw)
    bd = (lax.broadcasted_iota(jnp.int32, (GLA_WIDTH, GLA_KW), 0) // GLA_DV
          == lax.broadcasted_iota(jnp.int32, (GLA_WIDTH, GLA_KW), 1) // GLA_DK)
    if reverse:
        hr = lax.broadcasted_iota(jnp.int32, (GLA_WIDTH, GLA_WIDTH), 0) // GLA_DV
        hc = lax.broadcasted_iota(jnp.int32, (GLA_WIDTH, GLA_WIDTH), 1) // GLA_DV
        head_mean = jnp.where(hr == hc, 1.0 / GLA_DV, 0.0).astype(BF16)

    nchunk = tg // C
    order = range(nchunk - 1, -1, -1) if reverse else range(nchunk)
    d_off = GLA_KW if reverse else 0
    def intra(ci):
        r0 = ci * C
        q_in = gl_ref[0, r0:r0 + C, 0:128]
        k_in = gl_ref[0, r0:r0 + C, 128:256]
        k_st = gl_ref[0, r0:r0 + C, 256:384]
        v = v_ref[0, r0:r0 + C, :]
        kbd = jnp.concatenate([k_in * hm[h] for h in range(GLA_HEADS)], axis=0)
        return q_in, v, _dot_nt(q_in, kbd), _dot_tn(v, k_st)

    def combine(ci, S, q_in, v, att, cs):
        dec = dec_ref[0, ci:ci + 1, d_off:d_off + GLA_KW]
        att = jnp.where(cmask, att, 0.0).astype(BF16)
        vbd = jnp.concatenate([v * vm[h] for h in range(GLA_HEADS)], axis=0)
        o = _dot(att, vbd) + _dot_nt(q_in, S.astype(BF16))
        return o, S * dec + jnp.where(bd, cs, 0.0)

    def emit(ci, o):
        r0 = ci * C
        if reverse:
            ot = o + of_ref[0, r0:r0 + C, :]
            sh, sl = _split_bf16(ot * ot)
            msq = _dot(sh, head_mean) + _dot(sl, head_mean)
            y = ot * lax.rsqrt(msq + NORM_EPS) * nw_ref[...] * gc_ref[0, r0:r0 + C, :]
            o_ref[0, r0:r0 + C, :] = y.astype(BF16)
        else:
            o_ref[0, r0:r0 + C, :] = o

    order = list(order)
    ahead = min(GLA_LOOKAHEAD, nchunk)
    pending = [intra(ci) for ci in order[:ahead]]
    S = st_ref[...]
    prev = None
    for n, ci in enumerate(order):
        if n + ahead < nchunk:
            pending.append(intra(order[n + ahead]))
        o, S = combine(ci, S, *pending.pop(0))
        if prev is not None:
            emit(*prev)
        prev = (ci, o)
    emit(*prev)
    st_ref[...] = S

    @pl.when(t == nt - 1)
    def _():
        sfin_ref[0] = S


def _gla_pass(gl, v, dec, s0, reverse, tg, extra=None):
    B, S, _ = gl.shape
    nt = S // tg
    nch = tg // GLA_CHUNK
    if reverse:
        tmap = lambda b, t: (b, nt - 1 - t, 0)
    else:
        tmap = lambda b, t: (b, t, 0)
    tok = lambda w: pl.BlockSpec((1, tg, w), tmap)
    st_spec = pl.BlockSpec((1, GLA_WIDTH, GLA_KW), lambda b, t: (b, 0, 0))
    in_specs = [tok(384), tok(256), pl.BlockSpec((1, nch, 2 * GLA_KW), tmap), st_spec]
    args = [gl, v, dec, s0]
    if reverse:
        of, gc, nw = extra
        in_specs += [tok(256), tok(256), pl.BlockSpec((1, GLA_WIDTH), lambda b, t: (0, 0))]
        args += [of, gc, nw.reshape(1, GLA_WIDTH)]
    return pl.pallas_call(
        functools.partial(_gla_kernel, reverse=reverse, tg=tg, nt=nt),
        grid=(B, nt),
        in_specs=in_specs,
        out_specs=[tok(256), st_spec],
        out_shape=[jax.ShapeDtypeStruct((B, S, GLA_WIDTH), BF16 if reverse else F32),
                   jax.ShapeDtypeStruct((B, GLA_WIDTH, GLA_KW), F32)],
        scratch_shapes=[pltpu.VMEM((GLA_WIDTH, GLA_KW), F32)],
        compiler_params=_cparams(("parallel", "arbitrary"), 32),
        name="gla_bwd" if reverse else "gla_fwd",
    )(*args)


def _gla_branch(glf, glb, gv, dec, gc, norm_w, s0f, s0b, tg):
    of, sf = _gla_pass(glf, gv, dec, s0f, False, tg)
    c, sb = _gla_pass(glb, gv, dec, s0b, True, tg, extra=(of, gc, norm_w))
    return c, sf, sb


def _out_kernel(x_ref, a_ref, b_ref, c_ref, mod_ref, w_ref, fw_ref, o_ref, *, final):
    y = (_dot(a_ref[0], w_ref[0:256, :]) + _dot(b_ref[0], w_ref[256:768, :])
         + _dot(c_ref[0], w_ref[768:1024, :]))
    gate = mod_ref[:, 2 * D_MODEL:3 * D_MODEL]
    xn = x_ref[0] + gate * y
    if final:
        ms = jnp.mean(xn * xn, axis=-1, keepdims=True)
        xn = xn * lax.rsqrt(ms + NORM_EPS) * fw_ref[...]
    o_ref[0] = xn


def _out_proj(x, a, b, c, mod4, layer, mod_row, w_out, final_w, ts, final):
    B, S, _ = x.shape
    nt = S // ts
    if mod_row is None:
        mod_map = lambda bb, t: (layer, bb, 0, 0)
    else:
        mod_map = lambda bb, t: (layer, mod_row, 0, 0)
    tok = lambda w: pl.BlockSpec((1, ts, w), lambda bb, t: (bb, t, 0))
    return pl.pallas_call(
        functools.partial(_out_kernel, final=final),
        grid=(B, nt),
        in_specs=[tok(D_MODEL), tok(256), tok(512), tok(256),
                  pl.BlockSpec((None, None, 1, 3 * D_MODEL), mod_map),
                  pl.BlockSpec((D_MODEL, D_MODEL), lambda bb, t: (0, 0)),
                  pl.BlockSpec((1, D_MODEL), lambda bb, t: (0, 0))],
        out_specs=tok(D_MODEL),
        out_shape=jax.ShapeDtypeStruct((B, S, D_MODEL), F32),
        compiler_params=_cparams(("parallel", "arbitrary"), 40),
        name="out_proj",
    )(x, a, b, c, mod4, w_out, final_w.reshape(1, D_MODEL))


def _rope_tables(n_tokens):
    rows = n_tokens // GRID_W
    r = jnp.repeat(jnp.arange(rows, dtype=F32), GRID_W)
    col = jnp.tile(jnp.arange(GRID_W, dtype=F32), rows)
    n_freq = ATTN_HEAD_DIM // 4
    inv = ROPE_THETA ** (-jnp.arange(n_freq, dtype=F32) / n_freq)
    ang = jnp.concatenate([r[:, None] * inv, col[:, None] * inv], axis=-1)
    cos = jnp.tile(jnp.cos(ang), (1, 4))
    sin = jnp.sin(ang)
    return cos, jnp.concatenate([-sin, -sin, sin, sin], axis=-1)


def _split_heads(w, n_pairs):
    lead = w.shape[:-1]
    w = w.reshape(lead + (n_pairs, 2, 2, 32))
    return jnp.swapaxes(w, -2, -3).reshape(lead + (n_pairs * LANES,))


def _relayout_w_in(w_in):
    pad = jnp.zeros(w_in.shape[:-1] + (N_IN_PAD - N_IN,), F32)
    return jnp.concatenate([
        w_in[..., 0:768],
        _split_heads(w_in[..., 768:1280], 4),
        _split_heads(w_in[..., 1280:1408], 1),
        w_in[..., 1408:2560],
        w_in[..., 2592:2848],
        w_in[..., 2560:2592],
        pad], axis=-1).astype(BF16)


def _cumsum_matrices():
    n = GLA_GROUP // GLA_CHUNK
    low = np.kron(np.eye(n), np.tril(np.ones((GLA_CHUNK, GLA_CHUNK))))
    return jnp.asarray(np.stack([low, low.T]), BF16)


def kernel(x, c, ctx, c_ctx, w_mod, b_mod, w_in, conv_w, conv_b, conv_ln_w, conv_ln_b, attn_sink,
           gla_w_up, gla_b_up, gla_norm_w, w_out, final_norm_w):
    B, S, D = x.shape
    NC = ctx.shape[1]
    TS = 512

    w_p = _relayout_w_in(w_in)
    w_o = w_out.astype(BF16)
    w_up2 = jnp.zeros((DEPTH, LANES, 256), F32)
    w_up2 = w_up2.at[:, 0:GLA_LOW_RANK, 0:128].set(gla_w_up[:, 0])
    w_up2 = w_up2.at[:, GLA_LOW_RANK:2 * GLA_LOW_RANK, 128:256].set(gla_w_up[:, 1])
    wuh = w_up2.astype(BF16)
    wul = (w_up2 - wuh.astype(F32)).astype(BF16)
    bup = gla_b_up.reshape(DEPTH, 1, 256)
    tri = _cumsum_matrices()
    cos_l, sin_l = _rope_tables(S)
    cos_c = jnp.ones((NC, LANES), F32)
    sin_c = jnp.zeros((NC, LANES), F32)
    zero_state = jnp.zeros((B, GLA_WIDTH, GLA_KW), F32)

    cc = jnp.concatenate([c, c_ctx[None, :], jnp.zeros((16 - B - 1, D), F32)], axis=0)
    mod4 = _modulation(cc, w_mod, b_mod).reshape(DEPTH, 16, 1, 3 * D)
    CTX_ROW = B

    xl, xc = x, ctx
    for i in range(DEPTH):
        last = i == DEPTH - 1
        (u_c, q_c, kk_c, vt_c, ga_c, gb_c, gc_c, glf_c, glb_c, gv_c, dec_c) = _project(
            xc, mod4, i, CTX_ROW, cos_c, sin_c, w_p[i], wuh[i], wul[i], bup[i], tri, NC)
        c_c, s_f, s_b = _gla_branch(glf_c, glb_c, gv_c, dec_c, gc_c, gla_norm_w[i], zero_state, zero_state, NC)
        if not last:
            a_c = _conv_branch(u_c, ga_c, conv_w[i], conv_b[i], conv_ln_w[i], conv_ln_b[i], NC)
            b_c = _attention(q_c, kk_c, vt_c, kk_c, vt_c, gb_c, attn_sink[i], NC, False)
            xc = _out_proj(xc, a_c, b_c, c_c, mod4, i, CTX_ROW, w_o[i], final_norm_w, NC, False)
        (u_l, q_l, kk_l, vt_l, ga_l, gb_l, gc_l, glf_l, glb_l, gv_l, dec_l) = _project(
            xl, mod4, i, None, cos_l, sin_l, w_p[i], wuh[i], wul[i], bup[i], tri, TS)
        a_l = _conv_branch(u_l, ga_l, conv_w[i], conv_b[i], conv_ln_w[i], conv_ln_b[i], TS)
        b_l = _attention(q_l, kk_l, vt_l, kk_c, vt_c, gb_l, attn_sink[i], TS, True)
        c_l, _, _ = _gla_branch(glf_l, glb_l, gv_l, dec_l, gc_l, gla_norm_w[i], s_f, s_b, TS)
        xl = _out_proj(xl, a_l, b_l, c_l, mod4, i, None, w_o[i], final_norm_w, TS, last)
    return xl
```

```python
import functools
import math

import numpy as np
import jax
import jax.numpy as jnp
from jax import lax
from jax.experimental import pallas as pl
from jax.experimental.pallas import tpu as pltpu

F32 = jnp.float32
BF16 = jnp.bfloat16

D_MODEL = 1024
DEPTH = 2
GRID_W = 64
CONV_WIDTH = 256
CONV_KERNEL = 31
CONV_HALO = 16
ATTN_WIDTH = 512
ATTN_HEAD_DIM = 64
ATTN_HEADS = 8
ATTN_KV_HEADS = 2
ATTN_BLOCK = 128
ATTN_LOOKAHEAD = 2
ROPE_THETA = 10000.0
GLA_WIDTH = 256
GLA_HEADS = 4
GLA_DV = 64
GLA_DK = 32
GLA_KW = GLA_HEADS * GLA_DK
GLA_LOW_RANK = 16
GLA_TAU = 16.0
GLA_CHUNK = 64
GLA_LOOKAHEAD = 2
TAIL_PARTS = 2
GLA_GROUP = 256
NORM_EPS = 1e-6
NEG_INF = -1e30
PASS_CAP = 3.0e38
LOG2E = math.log2(math.e)
LANES = 128
SUBLANES = 8

N_IN = 2848
N_IN_PAD = 2944

P_A = 0
P_Q = 768
P_KV = 1280
P_BG = 1536
P_C = 2048
P_CG = 2560
P_LR = 2816


def _sigmoid(x):
    return 1.0 / (1.0 + jnp.exp(-x))


def _silu(x):
    return x * _sigmoid(x)


def _split_bf16(x):
    hi = x.astype(BF16)
    lo = (x - hi.astype(F32)).astype(BF16)
    return hi, lo


def _dot(a, b):
    return jnp.dot(a, b, preferred_element_type=F32)


def _dot_nt(a, b):
    return lax.dot_general(a, b, (((1,), (1,)), ((), ())), preferred_element_type=F32)


def _dot_tn(a, b):
    return lax.dot_general(a, b, (((0,), (0,)), ((), ())), preferred_element_type=F32)


def _cparams(sem, vmem_mb):
    return pltpu.CompilerParams(dimension_semantics=sem, vmem_limit_bytes=vmem_mb * 1024 * 1024)


def _mod_kernel(cc_ref, w_ref, b_ref, o_ref):
    s = _silu(cc_ref[...])
    sh, sl = _split_bf16(s)
    wh, wl = _split_bf16(w_ref[0])
    o_ref[0] = _dot(sh, wh) + _dot(sl, wh) + _dot(sh, wl) + b_ref[0]


def _modulation(cc, w_mod, b_mod):
    nb = 3
    blk = 3 * D_MODEL // nb
    return pl.pallas_call(
        _mod_kernel,
        grid=(DEPTH, nb),
        in_specs=[pl.BlockSpec((16, D_MODEL), lambda l, j: (0, 0)),
                  pl.BlockSpec((1, D_MODEL, blk), lambda l, j: (l, 0, j)),
                  pl.BlockSpec((1, 1, blk), lambda l, j: (l, 0, j))],
        out_specs=pl.BlockSpec((1, 16, blk), lambda l, j: (l, 0, j)),
        out_shape=jax.ShapeDtypeStruct((DEPTH, 16, 3 * D_MODEL), F32),
        compiler_params=_cparams(("arbitrary", "arbitrary"), 32),
        name="modulation",
    )(cc, w_mod, b_mod.reshape(DEPTH, 1, 3 * D_MODEL))


def _proj_kernel(x_ref, mod_ref, cos_ref, sin_ref, w_ref, wuh_ref, wul_ref, bup_ref, tri_ref, sel_ref,
                 u_ref, q_ref, kk_ref, vt_ref, ga_ref, gb_ref, gc_ref, glf_ref, glb_ref, gv_ref, dec_ref):
    ts = x_ref.shape[1]
    x = x_ref[0]
    mod = mod_ref[...]
    shift = mod[:, 0:D_MODEL]
    scale = mod[:, D_MODEL:2 * D_MODEL]
    ms = jnp.mean(x * x, axis=-1, keepdims=True)
    h = x * lax.rsqrt(ms + NORM_EPS) * (1.0 + scale) + shift
    hb = h.astype(BF16)

    def proj(a, b):
        return _dot(hb, w_ref[:, a:b])

    zlr = proj(P_LR, P_LR + LANES)
    zh, zl = _split_bf16(zlr)

    zc = proj(P_C, P_C + 512)
    gq = zc[:, 0:128] * (GLA_DK ** -0.5)
    gk = zc[:, 128:256]
    gv_ref[0] = zc[:, 256:512].astype(BF16)

    za = proj(P_A, P_A + 768)
    u_ref[0] = za[:, 0:256] * _sigmoid(za[:, 256:512])
    ga_ref[0] = _silu(za[:, 512:768])

    wuh = wuh_ref[...]
    zup = _dot(zh, wuh) + _dot(zl, wuh) + _dot(zh, wul_ref[...]) + bup_ref[...]
    la = (jnp.minimum(zup, 0.0) - jnp.log(1.0 + jnp.exp(-jnp.abs(zup)))) * (1.0 / GLA_TAU)
    lah, lal = _split_bf16(la)

    cos = cos_ref[...]
    sin = sin_ref[...]

    def rope(t):
        return t * cos + pltpu.roll(t, 64, 1) * sin

    zq = proj(P_Q, P_Q + 512)
    for c in range(4):
        r = rope(zq[:, c * LANES:(c + 1) * LANES]) * (ATTN_HEAD_DIM ** -0.5 * LOG2E)
        q_ref[0, :, c * LANES:(c + 1) * LANES] = r.astype(BF16)

    zkv = proj(P_KV, P_KV + 256)
    lane = lax.broadcasted_iota(jnp.int32, (1, LANES), 1)
    even = (lane % 64) < 32
    k = rope(zkv[:, 0:LANES])
    kk_ref[0, :, 0:128] = jnp.where(even, k, pltpu.roll(k, 32, 1)).astype(BF16)
    kk_ref[0, :, 128:256] = jnp.where(even, pltpu.roll(k, 96, 1), k).astype(BF16)
    vt_ref[0] = zkv[:, LANES:2 * LANES].T.astype(BF16)

    nch = ts // GLA_CHUNK
    sel = sel_ref[...]
    for d, out_ref in enumerate((glf_ref, glb_ref)):
        tri = tri_ref[d]
        cols = slice(d * GLA_KW, (d + 1) * GLA_KW)
        b = jnp.concatenate(
            [_dot(tri, lah[r:r + GLA_GROUP, cols]) + _dot(tri, lal[r:r + GLA_GROUP, cols])
             for r in range(0, ts, GLA_GROUP)], axis=0)
        tot = _dot(sel, lah[:, cols]) + _dot(sel, lal[:, cols])
        b3 = b.reshape(nch, GLA_CHUNK, GLA_KW)
        bt = b3[:, 0:1, :] if d else b3[:, GLA_CHUNK - 1:GLA_CHUNK, :]
        enb = jnp.exp(-b3)
        q_in = gq.reshape(nch, GLA_CHUNK, GLA_KW) * jnp.exp(b3)
        gk3 = gk.reshape(nch, GLA_CHUNK, GLA_KW)
        k_in = gk3 * enb
        k_st = gk3 * jnp.exp(bt - b3)
        out_ref[0, :, 0:128] = q_in.reshape(ts, GLA_KW).astype(BF16)
        out_ref[0, :, 128:256] = k_in.reshape(ts, GLA_KW).astype(BF16)
        out_ref[0, :, 256:384] = k_st.reshape(ts, GLA_KW).astype(BF16)
        dec_ref[0, :, cols] = jnp.exp(tot)

    gb_ref[0] = _silu(proj(P_BG, P_BG + 512))
    gc_ref[0] = _silu(proj(P_CG, P_CG + 256))


def _project(x, mod4, layer, mod_row, cos, sin, w_p, wuh, wul, bup, tri, ts):
    B, S, _ = x.shape
    nt = S // ts
    nch = ts // GLA_CHUNK
    if mod_row is None:
        mod_map = lambda b, t: (layer, b, 0, 0)
    else:
        mod_map = lambda b, t: (layer, mod_row, 0, 0)
    sel = jnp.asarray(np.kron(np.eye(nch), np.ones((1, GLA_CHUNK))), BF16)
    tok = lambda w: pl.BlockSpec((1, ts, w), lambda b, t: (b, t, 0))
    full = lambda a: pl.BlockSpec(a.shape, lambda b, t: (0,) * a.ndim)
    tok_out = ((256, F32), (512, BF16), (256, BF16), None, (256, F32), (512, F32), (256, F32),
               (384, BF16), (384, BF16), (256, BF16), None)
    out_specs, out_shape = [], []
    for i, wd in enumerate(tok_out):
        if i == 3:
            out_specs.append(pl.BlockSpec((1, LANES, ts), lambda b, t: (b, 0, t)))
            out_shape.append(jax.ShapeDtypeStruct((B, LANES, S), BF16))
        elif i == 10:
            out_specs.append(pl.BlockSpec((1, nch, 2 * GLA_KW), lambda b, t: (b, t, 0)))
            out_shape.append(jax.ShapeDtypeStruct((B, S // GLA_CHUNK, 2 * GLA_KW), F32))
        else:
            out_specs.append(tok(wd[0]))
            out_shape.append(jax.ShapeDtypeStruct((B, S, wd[0]), wd[1]))
    return pl.pallas_call(
        _proj_kernel,
        grid=(B, nt),
        in_specs=[tok(D_MODEL),
                  pl.BlockSpec((None, None, 1, 3 * D_MODEL), mod_map),
                  pl.BlockSpec((ts, LANES), lambda b, t: (t, 0)),
                  pl.BlockSpec((ts, LANES), lambda b, t: (t, 0)),
                  full(w_p), full(wuh), full(wul), full(bup), full(tri), full(sel)],
        out_specs=out_specs,
        out_shape=out_shape,
        compiler_params=_cparams(("parallel", "arbitrary"), 56),
        name="in_proj",
    )(x, mod4, cos, sin, w_p, wuh, wul, bup, tri, sel)


def _conv_prepare(is_first, is_last, um_ref, up_ref, un_ref, buf_ref, sh_ref, tc):
    H = CONV_HALO
    buf_ref[0:H] = jnp.where(is_first, 0.0, up_ref[0])
    buf_ref[H:H + tc] = um_ref[0]
    buf_ref[H + tc:2 * H + tc] = jnp.where(is_last, 0.0, un_ref[0])
    n_sh = sh_ref.shape[1]
    for p in range(SUBLANES):
        sh_ref[p] = buf_ref[p:p + n_sh, :]


def _conv_rows(r, rows, sh_ref, ga_ref, w_ref, b_ref, lnw_ref, lnb_ref):
    off = CONV_HALO - CONV_KERNEL // 2
    acc = jnp.zeros((rows, CONV_WIDTH), F32)
    for k in range(CONV_KERNEL):
        s = off + k
        a0 = r + (s // SUBLANES) * SUBLANES
        acc = acc + w_ref[k:k + 1, :] * sh_ref[s % SUBLANES, a0:a0 + rows, :]
    acc = acc + b_ref[...]
    mu = jnp.mean(acc, axis=-1, keepdims=True)
    cen = acc - mu
    var = jnp.mean(cen * cen, axis=-1, keepdims=True)
    y = cen * lax.rsqrt(var + NORM_EPS) * lnw_ref[...] + lnb_ref[...]
    return (_silu(y) * ga_ref[0, r:r + rows, :]).astype(BF16)


def _attn_kernel(sink_ref, q_ref, km_ref, kp_ref, kn_ref, vm_ref, vp_ref, vn_ref, ck_ref, cv_ref, cap_ref,
                 gb_ref, o_ref, kwin_ref, vwin_ref, *, nsub, nblk, has_local):
    t = pl.program_id(1)
    A = ATTN_BLOCK
    G = ATTN_HEADS // ATTN_KV_HEADS
    lane = lax.broadcasted_iota(jnp.int32, (1, LANES), 1)
    m_even = jnp.where((lane % 64) < 32, 1.0, 0.0).astype(BF16)
    m_odd = jnp.where((lane % 64) < 32, 0.0, 1.0).astype(BF16)
    strip = lax.broadcasted_iota(jnp.int32, (1, G * A), 1) // A
    if has_local:
        kwin_ref[0:A] = kp_ref[0]
        kwin_ref[A:A + nsub * A] = km_ref[0]
        kwin_ref[A + nsub * A:2 * A + nsub * A] = kn_ref[0]
        vwin_ref[0] = vp_ref[0]
        for s in range(nsub):
            vwin_ref[1 + s] = vm_ref[0, :, s * A:(s + 1) * A]
        vwin_ref[nsub + 1] = vn_ref[0]

    def scores(j, g):
        r0 = j * A
        q0 = q_ref[0, r0:r0 + A, (2 * g) * LANES:(2 * g + 1) * LANES]
        q1 = q_ref[0, r0:r0 + A, (2 * g + 1) * LANES:(2 * g + 2) * LANES]
        qs = jnp.concatenate([q0 * m_even, q0 * m_odd, q1 * m_even, q1 * m_odd], axis=0)
        sc = _dot_nt(ck_ref[0, :, g * LANES:(g + 1) * LANES], qs)
        sl = None
        if has_local:
            sl = _dot_nt(kwin_ref[r0:r0 + 3 * A, g * LANES:(g + 1) * LANES], qs)
        return sc, sl

    def finish(j, g, sc, sl):
        r0 = j * A
        sink = jnp.zeros((1, G * A), F32)
        for i in range(G):
            sink = jnp.where(strip == i, sink_ref[G * g + i] * LOG2E, sink)
        m = jnp.maximum(jnp.max(sc, axis=0, keepdims=True), sink)
        if has_local:
            blk = t * nsub + j
            cap = cap_ref[jnp.where(blk == 0, 1, 0) + jnp.where(blk == nblk - 1, 2, 0)]
            sl = jnp.minimum(sl, jnp.concatenate([cap] * G, axis=1))
            m = jnp.maximum(m, jnp.max(sl, axis=0, keepdims=True))
        vc = cv_ref[0, g * 64:(g + 1) * 64, :]
        ones = jnp.ones((16, vc.shape[1]), BF16)
        ot = _dot(jnp.concatenate([vc, ones], axis=0), jnp.exp2(sc - m).astype(BF16))
        if has_local:
            vl = jnp.concatenate([vwin_ref[j + i, g * 64:(g + 1) * 64, :] for i in range(3)], axis=1)
            ones = jnp.ones((16, 3 * A), BF16)
            ot = ot + _dot(jnp.concatenate([vl, ones], axis=0), jnp.exp2(sl - m).astype(BF16))
        den = ot[64:65, :] + jnp.exp2(sink - m)
        o = ot[0:64, :] * (1.0 / den)
        for i in range(2):
            c = 2 * g + i
            pair = jnp.concatenate([o[:, (2 * i) * A:(2 * i + 1) * A], o[:, (2 * i + 1) * A:(2 * i + 2) * A]],
                                   axis=0)
            gate = gb_ref[0, r0:r0 + A, c * LANES:(c + 1) * LANES]
            o_ref[0, r0:r0 + A, c * LANES:(c + 1) * LANES] = (pair.T * gate).astype(BF16)

    units = [(j, g) for j in range(nsub) for g in range(ATTN_KV_HEADS)]
    ahead = min(ATTN_LOOKAHEAD, len(units))
    pending = [scores(*u) for u in units[:ahead]]
    for n, unit in enumerate(units):
        if n + ahead < len(units):
            pending.append(scores(*units[n + ahead]))
        finish(*unit, *pending.pop(0))


def _attn_caps():
    A = ATTN_BLOCK
    c = np.arange(3 * A)[:, None]
    q = np.arange(A)[None, :]
    band = (c >= q) & (c <= q + 2 * A)
    caps = [band, band & (c >= A), band & (c < 2 * A)]
    return jnp.asarray(np.stack([np.where(b, PASS_CAP, NEG_INF) for b in caps]), F32)


def _attention(q, kk, vt, ckk, cvt, gb, sink, tq, has_local):
    B, S, _ = q.shape
    nt = S // tq
    nsub = tq // ATTN_BLOCK
    nblk = S // ATTN_BLOCK
    A = ATTN_BLOCK
    nc = ckk.shape[1]
    assert not has_local or nblk >= 2
    caps = _attn_caps()
    prev = lambda t: jnp.maximum(t * nsub - 1, 0)
    nxt = lambda t: jnp.minimum((t + 1) * nsub, nblk - 1)
    return pl.pallas_call(
        functools.partial(_attn_kernel, nsub=nsub, nblk=nblk, has_local=has_local),
        grid=(B, nt),
        in_specs=[pl.BlockSpec(memory_space=pltpu.SMEM),
                  pl.BlockSpec((1, tq, ATTN_WIDTH), lambda b, t: (b, t, 0)),
                  pl.BlockSpec((1, tq, 256), lambda b, t: (b, t, 0)),
                  pl.BlockSpec((1, A, 256), lambda b, t: (b, prev(t), 0)),
                  pl.BlockSpec((1, A, 256), lambda b, t: (b, nxt(t), 0)),
                  pl.BlockSpec((1, LANES, tq), lambda b, t: (b, 0, t)),
                  pl.BlockSpec((1, LANES, A), lambda b, t: (b, 0, prev(t))),
                  pl.BlockSpec((1, LANES, A), lambda b, t: (b, 0, nxt(t))),
                  pl.BlockSpec((1, nc, 256), lambda b, t: (b, 0, 0)),
                  pl.BlockSpec((1, LANES, nc), lambda b, t: (b, 0, 0)),
                  pl.BlockSpec(caps.shape, lambda b, t: (0, 0, 0)),
                  pl.BlockSpec((1, tq, ATTN_WIDTH), lambda b, t: (b, t, 0))],
        out_specs=pl.BlockSpec((1, tq, ATTN_WIDTH), lambda b, t: (b, t, 0)),
        out_shape=jax.ShapeDtypeStruct((B, S, ATTN_WIDTH), BF16),
        scratch_shapes=[pltpu.VMEM((tq + 2 * A, 256), BF16),
                        pltpu.VMEM((nsub + 2, LANES, A), BF16)],
        compiler_params=_cparams(("parallel", "arbitrary"), 32),
        name="attention" if has_local else "ctx_attention",
    )(sink, q, kk, kk, kk, vt, vt, vt, ckk, cvt, caps, gb)


def _gla_scan(gl_ref, v_ref, dec_ref, S, reverse, tg, emit):
    C = GLA_CHUNK
    lane_k = lax.broadcasted_iota(jnp.int32, (1, GLA_KW), 1)
    lane_v = lax.broadcasted_iota(jnp.int32, (1, GLA_WIDTH), 1)
    hm = [jnp.where(lane_k // GLA_DK == h, 1.0, 0.0).astype(BF16) for h in range(GLA_HEADS)]
    vm = [jnp.where(lane_v // GLA_DV == h, 1.0, 0.0).astype(BF16) for h in range(GLA_HEADS)]
    rw = lax.broadcasted_iota(jnp.int32, (C, GLA_WIDTH), 0)
    sw = lax.broadcasted_iota(jnp.int32, (C, GLA_WIDTH), 1) % C
    cmask = (sw >= rw) if reverse else (sw <= rw)
    bd = (lax.broadcasted_iota(jnp.int32, (GLA_WIDTH, GLA_KW), 0) // GLA_DV
          == lax.broadcasted_iota(jnp.int32, (GLA_WIDTH, GLA_KW), 1) // GLA_DK)

    nchunk = tg // C
    order = range(nchunk - 1, -1, -1) if reverse else range(nchunk)
    d_off = GLA_KW if reverse else 0
    def intra(ci):
        r0 = ci * C
        q_in = gl_ref[0, r0:r0 + C, 0:128]
        k_in = gl_ref[0, r0:r0 + C, 128:256]
        k_st = gl_ref[0, r0:r0 + C, 256:384]
        v = v_ref[0, r0:r0 + C, :]
        kbd = jnp.concatenate([k_in * hm[h] for h in range(GLA_HEADS)], axis=0)
        return q_in, v, _dot_nt(q_in, kbd), _dot_tn(v, k_st)

    def combine(ci, S, q_in, v, att, cs):
        dec = dec_ref[0, ci:ci + 1, d_off:d_off + GLA_KW]
        att = jnp.where(cmask, att, 0.0).astype(BF16)
        vbd = jnp.concatenate([v * vm[h] for h in range(GLA_HEADS)], axis=0)
        o = _dot(att, vbd) + _dot_nt(q_in, S.astype(BF16))
        return o, S * dec + jnp.where(bd, cs, 0.0)

    order = list(order)
    ahead = min(GLA_LOOKAHEAD, nchunk)
    pending = [intra(ci) for ci in order[:ahead]]
    prev = None
    for n, ci in enumerate(order):
        if n + ahead < nchunk:
            pending.append(intra(order[n + ahead]))
        o, S = combine(ci, S, *pending.pop(0))
        if prev is not None:
            emit(*prev)
        prev = (ci, o)
    emit(*prev)
    return S


def _head_mean_matrix():
    hr = lax.broadcasted_iota(jnp.int32, (GLA_WIDTH, GLA_WIDTH), 0) // GLA_DV
    hc = lax.broadcasted_iota(jnp.int32, (GLA_WIDTH, GLA_WIDTH), 1) // GLA_DV
    return jnp.where(hr == hc, 1.0 / GLA_DV, 0.0).astype(BF16)


def _gla_norm_gate(ot, head_mean, nw, gate):
    sh, sl = _split_bf16(ot * ot)
    msq = _dot(sh, head_mean) + _dot(sl, head_mean)
    return (ot * lax.rsqrt(msq + NORM_EPS) * nw * gate).astype(BF16)


def _gla_kernel(*refs, reverse, tg, nt):
    if reverse:
        gl_ref, v_ref, dec_ref, s0_ref, of_ref, gc_ref, nw_ref, o_ref, sfin_ref, st_ref = refs
        head_mean = _head_mean_matrix()
    else:
        gl_ref, v_ref, dec_ref, s0_ref, o_ref, sfin_ref, st_ref = refs
    t = pl.program_id(1)
    C = GLA_CHUNK

    @pl.when(t == 0)
    def _():
        st_ref[...] = s0_ref[0]

    def emit(ci, o):
        r0 = ci * C
        if reverse:
            ot = o + of_ref[0, r0:r0 + C, :]
            o_ref[0, r0:r0 + C, :] = _gla_norm_gate(ot, head_mean, nw_ref[...], gc_ref[0, r0:r0 + C, :])
        else:
            o_ref[0, r0:r0 + C, :] = o

    S = _gla_scan(gl_ref, v_ref, dec_ref, st_ref[...], reverse, tg, emit)
    st_ref[...] = S

    @pl.when(t == nt - 1)
    def _():
        sfin_ref[0] = S


def _gla_pass(gl, v, dec, s0, reverse, tg, extra=None):
    B, S, _ = gl.shape
    nt = S // tg
    nch = tg // GLA_CHUNK
    if reverse:
        tmap = lambda b, t: (b, nt - 1 - t, 0)
    else:
        tmap = lambda b, t: (b, t, 0)
    tok = lambda w: pl.BlockSpec((1, tg, w), tmap)
    st_spec = pl.BlockSpec((1, GLA_WIDTH, GLA_KW), lambda b, t: (b, 0, 0))
    in_specs = [tok(384), tok(256), pl.BlockSpec((1, nch, 2 * GLA_KW), tmap), st_spec]
    args = [gl, v, dec, s0]
    if reverse:
        of, gc, nw = extra
        in_specs += [tok(256), tok(256), pl.BlockSpec((1, GLA_WIDTH), lambda b, t: (0, 0))]
        args += [of, gc, nw.reshape(1, GLA_WIDTH)]
    return pl.pallas_call(
        functools.partial(_gla_kernel, reverse=reverse, tg=tg, nt=nt),
        grid=(B, nt),
        in_specs=in_specs,
        out_specs=[tok(256), st_spec],
        out_shape=[jax.ShapeDtypeStruct((B, S, GLA_WIDTH), BF16 if reverse else F32),
                   jax.ShapeDtypeStruct((B, GLA_WIDTH, GLA_KW), F32)],
        scratch_shapes=[pltpu.VMEM((GLA_WIDTH, GLA_KW), F32)],
        compiler_params=_cparams(("parallel", "arbitrary"), 32),
        name="gla_bwd" if reverse else "gla_fwd",
    )(*args)


def _tail_kernel(gl_ref, v_ref, dec_ref, s0_ref, of_ref, gc_ref, nw_ref,
                 um_ref, up_ref, un_ref, ga_ref, cw_ref, cb_ref, lnw_ref, lnb_ref,
                 b_ref, x_ref, mod_ref, w_ref, fw_ref,
                 o_ref, sfin_ref, st_ref, buf_ref, sh_ref, *, ts, nt, final):
    t = pl.program_id(1)
    C = GLA_CHUNK
    nchunk = ts // C
    part = max(nchunk // TAIL_PARTS, 1)

    @pl.when(t == 0)
    def _():
        st_ref[...] = s0_ref[0]

    _conv_prepare(t == nt - 1, t == 0, um_ref, up_ref, un_ref, buf_ref, sh_ref, ts)
    yb = _dot(b_ref[0], w_ref[256:768, :])
    head_mean = _head_mean_matrix()
    gate = mod_ref[:, 2 * D_MODEL:3 * D_MODEL]
    a_blk, c_blk = {}, {}

    def emit(ci, o):
        r0 = ci * C
        ot = o + of_ref[0, r0:r0 + C, :]
        c_blk[ci] = _gla_norm_gate(ot, head_mean, nw_ref[...], gc_ref[0, r0:r0 + C, :])
        a_blk[ci] = _conv_rows(r0, C, sh_ref, ga_ref, cw_ref, cb_ref, lnw_ref, lnb_ref)
        if ci % part == 0:
            grp = range(ci, ci + part)
            rows = slice(ci * C, (ci + part) * C)
            y = (yb[rows] + _dot(jnp.concatenate([a_blk[i] for i in grp], axis=0), w_ref[0:256, :])
                 + _dot(jnp.concatenate([c_blk[i] for i in grp], axis=0), w_ref[768:1024, :]))
            xn = x_ref[0, rows, :] + gate * y
            if final:
                ms = jnp.mean(xn * xn, axis=-1, keepdims=True)
                xn = xn * lax.rsqrt(ms + NORM_EPS) * fw_ref[...]
            o_ref[0, rows, :] = xn

    S = _gla_scan(gl_ref, v_ref, dec_ref, st_ref[...], True, ts, emit)
    st_ref[...] = S

    @pl.when(t == nt - 1)
    def _():
        sfin_ref[0] = S


def _layer_tail(glb, gv, dec, s0b, of, gc, norm_w, u, ga, conv_w, conv_b, ln_w, ln_b, b, x, mod4, layer, mod_row,
                w_out, final_w, ts, final):
    B, S, _ = x.shape
    nt = S // ts
    nch = ts // GLA_CHUNK
    hb = ts // CONV_HALO
    nhb = S // CONV_HALO
    n_sh = ts + 2 * CONV_HALO - SUBLANES
    tile = lambda t: nt - 1 - t
    if mod_row is None:
        mod_map = lambda bb, t: (layer, bb, 0, 0)
    else:
        mod_map = lambda bb, t: (layer, mod_row, 0, 0)
    tok = lambda w: pl.BlockSpec((1, ts, w), lambda bb, t: (bb, tile(t), 0))
    st_spec = pl.BlockSpec((1, GLA_WIDTH, GLA_KW), lambda bb, t: (bb, 0, 0))
    row = lambda a: a.reshape(1, -1)
    vec = lambda w: pl.BlockSpec((1, w), lambda bb, t: (0, 0))
    halo_prev = pl.BlockSpec((1, CONV_HALO, CONV_WIDTH), lambda bb, t: (bb, jnp.maximum(tile(t) * hb - 1, 0), 0))
    halo_next = pl.BlockSpec((1, CONV_HALO, CONV_WIDTH),
                             lambda bb, t: (bb, jnp.minimum((tile(t) + 1) * hb, nhb - 1), 0))
    return pl.pallas_call(
        functools.partial(_tail_kernel, ts=ts, nt=nt, final=final),
        grid=(B, nt),
        in_specs=[tok(384), tok(256), pl.BlockSpec((1, nch, 2 * GLA_KW), lambda bb, t: (bb, tile(t), 0)), st_spec,
                  tok(256), tok(256), vec(GLA_WIDTH),
                  tok(CONV_WIDTH), halo_prev, halo_next, tok(CONV_WIDTH),
                  pl.BlockSpec((CONV_KERNEL, CONV_WIDTH), lambda bb, t: (0, 0)),
                  vec(CONV_WIDTH), vec(CONV_WIDTH), vec(CONV_WIDTH),
                  tok(ATTN_WIDTH), tok(D_MODEL),
                  pl.BlockSpec((None, None, 1, 3 * D_MODEL), mod_map),
                  pl.BlockSpec((D_MODEL, D_MODEL), lambda bb, t: (0, 0)),
                  vec(D_MODEL)],
        out_specs=[tok(D_MODEL), st_spec],
        out_shape=[jax.ShapeDtypeStruct((B, S, D_MODEL), F32),
                   jax.ShapeDtypeStruct((B, GLA_WIDTH, GLA_KW), F32)],
        scratch_shapes=[pltpu.VMEM((GLA_WIDTH, GLA_KW), F32),
                        pltpu.VMEM((ts + 2 * CONV_HALO, CONV_WIDTH), F32),
                        pltpu.VMEM((SUBLANES, n_sh, CONV_WIDTH), F32)],
        compiler_params=_cparams(("parallel", "arbitrary"), 48),
        name="layer_tail",
    )(glb, gv, dec, s0b, of, gc, row(norm_w), u, u, u, ga, conv_w, row(conv_b), row(ln_w), row(ln_b),
      b, x, mod4, w_out, row(final_w))


def _rope_tables(n_tokens):
    rows = n_tokens // GRID_W
    r = jnp.repeat(jnp.arange(rows, dtype=F32), GRID_W)
    col = jnp.tile(jnp.arange(GRID_W, dtype=F32), rows)
    n_freq = ATTN_HEAD_DIM // 4
    inv = ROPE_THETA ** (-jnp.arange(n_freq, dtype=F32) / n_freq)
    ang = jnp.concatenate([r[:, None] * inv, col[:, None] * inv], axis=-1)
    cos = jnp.tile(jnp.cos(ang), (1, 4))
    sin = jnp.sin(ang)
    return cos, jnp.concatenate([-sin, -sin, sin, sin], axis=-1)


def _split_heads(w, n_pairs):
    lead = w.shape[:-1]
    w = w.reshape(lead + (n_pairs, 2, 2, 32))
    return jnp.swapaxes(w, -2, -3).reshape(lead + (n_pairs * LANES,))


def _relayout_w_in(w_in):
    pad = jnp.zeros(w_in.shape[:-1] + (N_IN_PAD - N_IN,), F32)
    return jnp.concatenate([
        w_in[..., 0:768],
        _split_heads(w_in[..., 768:1280], 4),
        _split_heads(w_in[..., 1280:1408], 1),
        w_in[..., 1408:2560],
        w_in[..., 2592:2848],
        w_in[..., 2560:2592],
        pad], axis=-1).astype(BF16)


def _cumsum_matrices():
    n = GLA_GROUP // GLA_CHUNK
    low = np.kron(np.eye(n), np.tril(np.ones((GLA_CHUNK, GLA_CHUNK))))
    return jnp.asarray(np.stack([low, low.T]), BF16)


def kernel(x, c, ctx, c_ctx, w_mod, b_mod, w_in, conv_w, conv_b, conv_ln_w, conv_ln_b, attn_sink,
           gla_w_up, gla_b_up, gla_norm_w, w_out, final_norm_w):
    B, S, D = x.shape
    NC = ctx.shape[1]
    TS = 512

    w_p = _relayout_w_in(w_in)
    w_o = w_out.astype(BF16)
    w_up2 = jnp.zeros((DEPTH, LANES, 256), F32)
    w_up2 = w_up2.at[:, 0:GLA_LOW_RANK, 0:128].set(gla_w_up[:, 0])
    w_up2 = w_up2.at[:, GLA_LOW_RANK:2 * GLA_LOW_RANK, 128:256].set(gla_w_up[:, 1])
    wuh = w_up2.astype(BF16)
    wul = (w_up2 - wuh.astype(F32)).astype(BF16)
    bup = gla_b_up.reshape(DEPTH, 1, 256)
    tri = _cumsum_matrices()
    cos_l, sin_l = _rope_tables(S)
    cos_c = jnp.ones((NC, LANES), F32)
    sin_c = jnp.zeros((NC, LANES), F32)
    zero_state = jnp.zeros((B, GLA_WIDTH, GLA_KW), F32)

    cc = jnp.concatenate([c, c_ctx[None, :], jnp.zeros((16 - B - 1, D), F32)], axis=0)
    mod4 = _modulation(cc, w_mod, b_mod).reshape(DEPTH, 16, 1, 3 * D)
    CTX_ROW = B

    xl, xc = x, ctx
    for i in range(DEPTH):
        last = i == DEPTH - 1
        (u_c, q_c, kk_c, vt_c, ga_c, gb_c, gc_c, glf_c, glb_c, gv_c, dec_c) = _project(
            xc, mod4, i, CTX_ROW, cos_c, sin_c, w_p[i], wuh[i], wul[i], bup[i], tri, NC)
        of_c, s_f = _gla_pass(glf_c, gv_c, dec_c, zero_state, False, NC)
        if last:
            _, s_b = _gla_pass(glb_c, gv_c, dec_c, zero_state, True, NC, extra=(of_c, gc_c, gla_norm_w[i]))
        else:
            b_c = _attention(q_c, kk_c, vt_c, kk_c, vt_c, gb_c, attn_sink[i], NC, False)
            xc, s_b = _layer_tail(glb_c, gv_c, dec_c, zero_state, of_c, gc_c, gla_norm_w[i], u_c, ga_c,
                                  conv_w[i], conv_b[i], conv_ln_w[i], conv_ln_b[i], b_c, xc, mod4, i, CTX_ROW,
                                  w_o[i], final_norm_w, NC, False)
        (u_l, q_l, kk_l, vt_l, ga_l, gb_l, gc_l, glf_l, glb_l, gv_l, dec_l) = _project(
            xl, mod4, i, None, cos_l, sin_l, w_p[i], wuh[i], wul[i], bup[i], tri, TS)
        b_l = _attention(q_l, kk_l, vt_l, kk_c, vt_c, gb_l, attn_sink[i], TS, True)
        of_l, _ = _gla_pass(glf_l, gv_l, dec_l, s_f, False, TS)
        xl, _ = _layer_tail(glb_l, gv_l, dec_l, s_b, of_l, gc_l, gla_norm_w[i], u_l, ga_l,
                            conv_w[i], conv_b[i], conv_ln_w[i], conv_ln_b[i], b_l, xl, mod4, i, None,
                            w_o[i], final_norm_w, TS, last)
    return xl
```

```python
import functools
import math

import numpy as np
import jax
import jax.numpy as jnp
from jax import lax
from jax.experimental import pallas as pl
from jax.experimental.pallas import tpu as pltpu

F32 = jnp.float32
BF16 = jnp.bfloat16

D_MODEL = 1024
DEPTH = 2
GRID_W = 64
CONV_WIDTH = 256
CONV_KERNEL = 31
CONV_HALO = 16
ATTN_WIDTH = 512
ATTN_HEAD_DIM = 64
ATTN_HEADS = 8
ATTN_KV_HEADS = 2
ATTN_BLOCK = 128
ATTN_LOOKAHEAD = 2
ROPE_THETA = 10000.0
GLA_WIDTH = 256
GLA_HEADS = 4
GLA_DV = 64
GLA_DK = 32
GLA_KW = GLA_HEADS * GLA_DK
GLA_LOW_RANK = 16
GLA_TAU = 16.0
GLA_CHUNK = 64
GLA_LOOKAHEAD = 2
TAIL_PARTS = 2
GLA_GROUP = 256
NORM_EPS = 1e-6
NEG_INF = -1e30
PASS_CAP = 3.0e38
LOG2E = math.log2(math.e)
LANES = 128
SUBLANES = 8

N_IN = 2848
N_IN_PAD = 2944

P_A = 0
P_Q = 768
P_KV = 1280
P_BG = 1536
P_C = 2048
P_CG = 2560
P_LR = 2816


def _sigmoid(x):
    return 1.0 / (1.0 + jnp.exp(-x))


def _silu(x):
    return x * _sigmoid(x)


def _split_bf16(x):
    hi = x.astype(BF16)
    lo = (x - hi.astype(F32)).astype(BF16)
    return hi, lo


def _dot(a, b):
    return jnp.dot(a, b, preferred_element_type=F32)


def _dot_nt(a, b):
    return lax.dot_general(a, b, (((1,), (1,)), ((), ())), preferred_element_type=F32)


def _dot_tn(a, b):
    return lax.dot_general(a, b, (((0,), (0,)), ((), ())), preferred_element_type=F32)


def _cparams(sem, vmem_mb):
    return pltpu.CompilerParams(dimension_semantics=sem, vmem_limit_bytes=vmem_mb * 1024 * 1024)


def _mod_kernel(cc_ref, w_ref, b_ref, o_ref):
    s = _silu(cc_ref[...])
    sh, sl = _split_bf16(s)
    wh, wl = _split_bf16(w_ref[0])
    o_ref[0] = _dot(sh, wh) + _dot(sl, wh) + _dot(sh, wl) + b_ref[0]


def _modulation(cc, w_mod, b_mod):
    nb = 3
    blk = 3 * D_MODEL // nb
    return pl.pallas_call(
        _mod_kernel,
        grid=(DEPTH, nb),
        in_specs=[pl.BlockSpec((16, D_MODEL), lambda l, j: (0, 0)),
                  pl.BlockSpec((1, D_MODEL, blk), lambda l, j: (l, 0, j)),
                  pl.BlockSpec((1, 1, blk), lambda l, j: (l, 0, j))],
        out_specs=pl.BlockSpec((1, 16, blk), lambda l, j: (l, 0, j)),
        out_shape=jax.ShapeDtypeStruct((DEPTH, 16, 3 * D_MODEL), F32),
        compiler_params=_cparams(("arbitrary", "arbitrary"), 32),
        name="modulation",
    )(cc, w_mod, b_mod.reshape(DEPTH, 1, 3 * D_MODEL))


def _proj_kernel(x_ref, xp_ref, xn_ref, mod_ref, cos_ref, sin_ref, w_ref, wuh_ref, wul_ref, bup_ref, tri_ref,
                 sel_ref, cw_ref, cb_ref, lnw_ref, lnb_ref,
                 a_ref, q_ref, kk_ref, vt_ref, gb_ref, gc_ref, glf_ref, glb_ref, gv_ref, dec_ref,
                 buf_ref, sh_ref, *, nt):
    t = pl.program_id(1)
    ts = x_ref.shape[1]
    H = CONV_HALO
    mod = mod_ref[...]
    shift = mod[:, 0:D_MODEL]
    scale = mod[:, D_MODEL:2 * D_MODEL]

    def norm_mod(x):
        ms = jnp.mean(x * x, axis=-1, keepdims=True)
        return (x * lax.rsqrt(ms + NORM_EPS) * (1.0 + scale) + shift).astype(BF16)

    hb = norm_mod(x_ref[0])

    def proj(a, b):
        return _dot(hb, w_ref[:, a:b])

    zlr = proj(P_LR, P_LR + LANES)
    zh, zl = _split_bf16(zlr)

    hb_ext = jnp.concatenate([norm_mod(xp_ref[0]), hb, norm_mod(xn_ref[0])], axis=0)
    zu = _dot(hb_ext, w_ref[:, P_A:P_A + 512])
    u_ext = zu[:, 0:256] * _sigmoid(zu[:, 256:512])
    buf_ref[...] = u_ext
    buf_ref[0:H] = jnp.where(t == 0, 0.0, u_ext[0:H])
    buf_ref[H + ts:2 * H + ts] = jnp.where(t == nt - 1, 0.0, u_ext[H + ts:2 * H + ts])
    _conv_phase_copies(buf_ref, sh_ref)
    zga = proj(P_A + 512, P_A + 768)
    ga = _silu(zga)

    def conv_rows(r0, r1, after):
        bits = lax.bitcast_convert_type(after[0:1, 0:CONV_WIDTH], jnp.int32)
        zero = lax.shift_right_logical(lax.shift_right_logical(bits, 16), 16).astype(F32)
        w_first = cw_ref[0:1, :] + zero
        for r in range(r0, r1, GLA_CHUNK):
            a_ref[0, r:r + GLA_CHUNK, :] = _conv_rows(r, GLA_CHUNK, sh_ref, ga[r:r + GLA_CHUNK], cw_ref, cb_ref,
                                                     lnw_ref, lnb_ref, w_first)

    zc = proj(P_C, P_C + 512)
    gq = zc[:, 0:128] * (GLA_DK ** -0.5)
    gk = zc[:, 128:256]
    gv_ref[0] = zc[:, 256:512].astype(BF16)
    conv_rows(0, ts // 4, zga)

    wuh = wuh_ref[...]
    zup = _dot(zh, wuh) + _dot(zl, wuh) + _dot(zh, wul_ref[...]) + bup_ref[...]
    la = (jnp.minimum(zup, 0.0) - jnp.log(1.0 + jnp.exp(-jnp.abs(zup)))) * (1.0 / GLA_TAU)
    lah, lal = _split_bf16(la)

    cos = cos_ref[...]
    sin = sin_ref[...]

    def rope(t):
        return t * cos + pltpu.roll(t, 64, 1) * sin

    zq = proj(P_Q, P_Q + 512)
    for c in range(4):
        r = rope(zq[:, c * LANES:(c + 1) * LANES]) * (ATTN_HEAD_DIM ** -0.5 * LOG2E)
        q_ref[0, :, c * LANES:(c + 1) * LANES] = r.astype(BF16)
    conv_rows(ts // 4, ts // 2, zc)

    zkv = proj(P_KV, P_KV + 256)
    lane = lax.broadcasted_iota(jnp.int32, (1, LANES), 1)
    even = (lane % 64) < 32
    k = rope(zkv[:, 0:LANES])
    kk_ref[0, :, 0:128] = jnp.where(even, k, pltpu.roll(k, 32, 1)).astype(BF16)
    kk_ref[0, :, 128:256] = jnp.where(even, pltpu.roll(k, 96, 1), k).astype(BF16)
    vt_ref[0] = zkv[:, LANES:2 * LANES].T.astype(BF16)
    conv_rows(ts // 2, 3 * ts // 4, zq)

    nch = ts // GLA_CHUNK
    sel = sel_ref[...]
    for d, out_ref in enumerate((glf_ref, glb_ref)):
        tri = tri_ref[d]
        cols = slice(d * GLA_KW, (d + 1) * GLA_KW)
        b = jnp.concatenate(
            [_dot(tri, lah[r:r + GLA_GROUP, cols]) + _dot(tri, lal[r:r + GLA_GROUP, cols])
             for r in range(0, ts, GLA_GROUP)], axis=0)
        tot = _dot(sel, lah[:, cols]) + _dot(sel, lal[:, cols])
        b3 = b.reshape(nch, GLA_CHUNK, GLA_KW)
        bt = b3[:, 0:1, :] if d else b3[:, GLA_CHUNK - 1:GLA_CHUNK, :]
        enb = jnp.exp(-b3)
        q_in = gq.reshape(nch, GLA_CHUNK, GLA_KW) * jnp.exp(b3)
        gk3 = gk.reshape(nch, GLA_CHUNK, GLA_KW)
        k_in = gk3 * enb
        k_st = gk3 * jnp.exp(bt - b3)
        out_ref[0, :, 0:128] = q_in.reshape(ts, GLA_KW).astype(BF16)
        out_ref[0, :, 128:256] = k_in.reshape(ts, GLA_KW).astype(BF16)
        out_ref[0, :, 256:384] = k_st.reshape(ts, GLA_KW).astype(BF16)
        dec_ref[0, :, cols] = jnp.exp(tot)

    gb_ref[0] = _silu(proj(P_BG, P_BG + 512)).astype(BF16)
    conv_rows(3 * ts // 4, ts, zkv)
    gc_ref[0] = _silu(proj(P_CG, P_CG + 256)).astype(BF16)


def _project(x, mod4, layer, mod_row, cos, sin, w_p, wuh, wul, bup, tri, conv_w, conv_b, ln_w, ln_b, ts):
    B, S, _ = x.shape
    nt = S // ts
    nch = ts // GLA_CHUNK
    hb = ts // CONV_HALO
    nhb = S // CONV_HALO
    row = lambda a: a.reshape(1, CONV_WIDTH)
    vec = pl.BlockSpec((1, CONV_WIDTH), lambda b, t: (0, 0))
    if mod_row is None:
        mod_map = lambda b, t: (layer, b, 0, 0)
    else:
        mod_map = lambda b, t: (layer, mod_row, 0, 0)
    sel = jnp.asarray(np.kron(np.eye(nch), np.ones((1, GLA_CHUNK))), BF16)
    tok = lambda w: pl.BlockSpec((1, ts, w), lambda b, t: (b, t, 0))
    full = lambda a: pl.BlockSpec(a.shape, lambda b, t: (0,) * a.ndim)
    widths = (256, 512, 256, None, 512, 256, 384, 384, 256)
    out_specs, out_shape = [], []
    for wd in widths:
        if wd is None:
            out_specs.append(pl.BlockSpec((1, LANES, ts), lambda b, t: (b, 0, t)))
            out_shape.append(jax.ShapeDtypeStruct((B, LANES, S), BF16))
        else:
            out_specs.append(tok(wd))
            out_shape.append(jax.ShapeDtypeStruct((B, S, wd), BF16))
    out_specs.append(pl.BlockSpec((1, nch, 2 * GLA_KW), lambda b, t: (b, t, 0)))
    out_shape.append(jax.ShapeDtypeStruct((B, S // GLA_CHUNK, 2 * GLA_KW), F32))
    halo = lambda idx: pl.BlockSpec((1, CONV_HALO, D_MODEL), lambda b, t: (b, idx(t), 0))
    return pl.pallas_call(
        functools.partial(_proj_kernel, nt=nt),
        grid=(B, nt),
        in_specs=[tok(D_MODEL),
                  halo(lambda t: jnp.maximum(t * hb - 1, 0)),
                  halo(lambda t: jnp.minimum((t + 1) * hb, nhb - 1)),
                  pl.BlockSpec((None, None, 1, 3 * D_MODEL), mod_map),
                  pl.BlockSpec((ts, LANES), lambda b, t: (t, 0)),
                  pl.BlockSpec((ts, LANES), lambda b, t: (t, 0)),
                  full(w_p), full(wuh), full(wul), full(bup), full(tri), full(sel),
                  pl.BlockSpec((CONV_KERNEL, CONV_WIDTH), lambda b, t: (0, 0)), vec, vec, vec],
        out_specs=out_specs,
        out_shape=out_shape,
        scratch_shapes=[pltpu.VMEM((ts + 2 * CONV_HALO, CONV_WIDTH), F32),
                        pltpu.VMEM((SUBLANES, ts + 2 * CONV_HALO - SUBLANES, CONV_WIDTH), F32)],
        compiler_params=_cparams(("parallel", "arbitrary"), 56),
        name="in_proj",
    )(x, x, x, mod4, cos, sin, w_p, wuh, wul, bup, tri, sel, conv_w, row(conv_b), row(ln_w), row(ln_b))


def _conv_phase_copies(buf_ref, sh_ref):
    n_sh = sh_ref.shape[1]
    for p in range(SUBLANES):
        sh_ref[p] = buf_ref[p:p + n_sh, :]


def _conv_rows(r, rows, sh_ref, gate, w_ref, b_ref, lnw_ref, lnb_ref, w_first):
    off = CONV_HALO - CONV_KERNEL // 2
    acc = jnp.zeros((rows, CONV_WIDTH), F32)
    for k in range(CONV_KERNEL):
        s = off + k
        a0 = r + (s // SUBLANES) * SUBLANES
        wk = w_first if k == 0 else w_ref[k:k + 1, :]
        acc = acc + wk * sh_ref[s % SUBLANES, a0:a0 + rows, :]
    acc = acc + b_ref[...]
    mu = jnp.mean(acc, axis=-1, keepdims=True)
    cen = acc - mu
    var = jnp.mean(cen * cen, axis=-1, keepdims=True)
    y = cen * lax.rsqrt(var + NORM_EPS) * lnw_ref[...] + lnb_ref[...]
    return (_silu(y) * gate).astype(BF16)


def _attn_kernel(sink_ref, q_ref, km_ref, kp_ref, kn_ref, vm_ref, vp_ref, vn_ref, ck_ref, cv_ref, cap_ref,
                 gb_ref, o_ref, kwin_ref, vwin_ref, *, nsub, nblk, has_local):
    t = pl.program_id(1)
    A = ATTN_BLOCK
    G = ATTN_HEADS // ATTN_KV_HEADS
    lane = lax.broadcasted_iota(jnp.int32, (1, LANES), 1)
    m_even = jnp.where((lane % 64) < 32, 1.0, 0.0).astype(BF16)
    m_odd = jnp.where((lane % 64) < 32, 0.0, 1.0).astype(BF16)
    strip = lax.broadcasted_iota(jnp.int32, (1, G * A), 1) // A
    if has_local:
        kwin_ref[0:A] = kp_ref[0]
        kwin_ref[A:A + nsub * A] = km_ref[0]
        kwin_ref[A + nsub * A:2 * A + nsub * A] = kn_ref[0]
        vwin_ref[0] = vp_ref[0]
        for s in range(nsub):
            vwin_ref[1 + s] = vm_ref[0, :, s * A:(s + 1) * A]
        vwin_ref[nsub + 1] = vn_ref[0]

    def scores(j, g):
        r0 = j * A
        q0 = q_ref[0, r0:r0 + A, (2 * g) * LANES:(2 * g + 1) * LANES]
        q1 = q_ref[0, r0:r0 + A, (2 * g + 1) * LANES:(2 * g + 2) * LANES]
        qs = jnp.concatenate([q0 * m_even, q0 * m_odd, q1 * m_even, q1 * m_odd], axis=0)
        sc = _dot_nt(ck_ref[0, :, g * LANES:(g + 1) * LANES], qs)
        sl = None
        if has_local:
            sl = _dot_nt(kwin_ref[r0:r0 + 3 * A, g * LANES:(g + 1) * LANES], qs)
        return sc, sl

    def finish(j, g, sc, sl):
        r0 = j * A
        sink = jnp.zeros((1, G * A), F32)
        for i in range(G):
            sink = jnp.where(strip == i, sink_ref[G * g + i] * LOG2E, sink)
        m = jnp.maximum(jnp.max(sc, axis=0, keepdims=True), sink)
        if has_local:
            blk = t * nsub + j
            cap = cap_ref[jnp.where(blk == 0, 1, 0) + jnp.where(blk == nblk - 1, 2, 0)]
            sl = jnp.minimum(sl, jnp.concatenate([cap] * G, axis=1))
            m = jnp.maximum(m, jnp.max(sl, axis=0, keepdims=True))
        vc = cv_ref[0, g * 64:(g + 1) * 64, :]
        ones = jnp.ones((16, vc.shape[1]), BF16)
        ot = _dot(jnp.concatenate([vc, ones], axis=0), jnp.exp2(sc - m).astype(BF16))
        if has_local:
            vl = jnp.concatenate([vwin_ref[j + i, g * 64:(g + 1) * 64, :] for i in range(3)], axis=1)
            ones = jnp.ones((16, 3 * A), BF16)
            ot = ot + _dot(jnp.concatenate([vl, ones], axis=0), jnp.exp2(sl - m).astype(BF16))
        den = ot[64:65, :] + jnp.exp2(sink - m)
        o = ot[0:64, :] * (1.0 / den)
        for i in range(2):
            c = 2 * g + i
            pair = jnp.concatenate([o[:, (2 * i) * A:(2 * i + 1) * A], o[:, (2 * i + 1) * A:(2 * i + 2) * A]],
                                   axis=0)
            gate = gb_ref[0, r0:r0 + A, c * LANES:(c + 1) * LANES]
            o_ref[0, r0:r0 + A, c * LANES:(c + 1) * LANES] = (pair.T * gate).astype(BF16)

    units = [(j, g) for j in range(nsub) for g in range(ATTN_KV_HEADS)]
    ahead = min(ATTN_LOOKAHEAD, len(units))
    pending = [scores(*u) for u in units[:ahead]]
    for n, unit in enumerate(units):
        if n + ahead < len(units):
            pending.append(scores(*units[n + ahead]))
        finish(*unit, *pending.pop(0))


def _attn_caps():
    A = ATTN_BLOCK
    c = np.arange(3 * A)[:, None]
    q = np.arange(A)[None, :]
    band = (c >= q) & (c <= q + 2 * A)
    caps = [band, band & (c >= A), band & (c < 2 * A)]
    return jnp.asarray(np.stack([np.where(b, PASS_CAP, NEG_INF) for b in caps]), F32)


def _attention(q, kk, vt, ckk, cvt, gb, sink, tq, has_local):
    B, S, _ = q.shape
    nt = S // tq
    nsub = tq // ATTN_BLOCK
    nblk = S // ATTN_BLOCK
    A = ATTN_BLOCK
    nc = ckk.shape[1]
    assert not has_local or nblk >= 2
    caps = _attn_caps()
    prev = lambda t: jnp.maximum(t * nsub - 1, 0)
    nxt = lambda t: jnp.minimum((t + 1) * nsub, nblk - 1)
    return pl.pallas_call(
        functools.partial(_attn_kernel, nsub=nsub, nblk=nblk, has_local=has_local),
        grid=(B, nt),
        in_specs=[pl.BlockSpec(memory_space=pltpu.SMEM),
                  pl.BlockSpec((1, tq, ATTN_WIDTH), lambda b, t: (b, t, 0)),
                  pl.BlockSpec((1, tq, 256), lambda b, t: (b, t, 0)),
                  pl.BlockSpec((1, A, 256), lambda b, t: (b, prev(t), 0)),
                  pl.BlockSpec((1, A, 256), lambda b, t: (b, nxt(t), 0)),
                  pl.BlockSpec((1, LANES, tq), lambda b, t: (b, 0, t)),
                  pl.BlockSpec((1, LANES, A), lambda b, t: (b, 0, prev(t))),
                  pl.BlockSpec((1, LANES, A), lambda b, t: (b, 0, nxt(t))),
                  pl.BlockSpec((1, nc, 256), lambda b, t: (b, 0, 0)),
                  pl.BlockSpec((1, LANES, nc), lambda b, t: (b, 0, 0)),
                  pl.BlockSpec(caps.shape, lambda b, t: (0, 0, 0)),
                  pl.BlockSpec((1, tq, ATTN_WIDTH), lambda b, t: (b, t, 0))],
        out_specs=pl.BlockSpec((1, tq, ATTN_WIDTH), lambda b, t: (b, t, 0)),
        out_shape=jax.ShapeDtypeStruct((B, S, ATTN_WIDTH), BF16),
        scratch_shapes=[pltpu.VMEM((tq + 2 * A, 256), BF16),
                        pltpu.VMEM((nsub + 2, LANES, A), BF16)],
        compiler_params=_cparams(("parallel", "arbitrary"), 32),
        name="attention" if has_local else "ctx_attention",
    )(sink, q, kk, kk, kk, vt, vt, vt, ckk, cvt, caps, gb)


def _gla_scan(gl_ref, v_ref, dec_ref, S, reverse, tg, emit):
    C = GLA_CHUNK
    lane_k = lax.broadcasted_iota(jnp.int32, (1, GLA_KW), 1)
    lane_v = lax.broadcasted_iota(jnp.int32, (1, GLA_WIDTH), 1)
    hm = [jnp.where(lane_k // GLA_DK == h, 1.0, 0.0).astype(BF16) for h in range(GLA_HEADS)]
    vm = [jnp.where(lane_v // GLA_DV == h, 1.0, 0.0).astype(BF16) for h in range(GLA_HEADS)]
    rw = lax.broadcasted_iota(jnp.int32, (C, GLA_WIDTH), 0)
    sw = lax.broadcasted_iota(jnp.int32, (C, GLA_WIDTH), 1) % C
    cmask = (sw >= rw) if reverse else (sw <= rw)
    bd = (lax.broadcasted_iota(jnp.int32, (GLA_WIDTH, GLA_KW), 0) // GLA_DV
          == lax.broadcasted_iota(jnp.int32, (GLA_WIDTH, GLA_KW), 1) // GLA_DK)

    nchunk = tg // C
    order = range(nchunk - 1, -1, -1) if reverse else range(nchunk)
    d_off = GLA_KW if reverse else 0
    def intra(ci):
        r0 = ci * C
        q_in = gl_ref[0, r0:r0 + C, 0:128]
        k_in = gl_ref[0, r0:r0 + C, 128:256]
        k_st = gl_ref[0, r0:r0 + C, 256:384]
        v = v_ref[0, r0:r0 + C, :]
        kbd = jnp.concatenate([k_in * hm[h] for h in range(GLA_HEADS)], axis=0)
        return q_in, v, _dot_nt(q_in, kbd), _dot_tn(v, k_st)

    def combine(ci, S, q_in, v, att, cs):
        dec = dec_ref[0, ci:ci + 1, d_off:d_off + GLA_KW]
        att = jnp.where(cmask, att, 0.0).astype(BF16)
        vbd = jnp.concatenate([v * vm[h] for h in range(GLA_HEADS)], axis=0)
        o = _dot(att, vbd) + _dot_nt(q_in, S.astype(BF16))
        return o, S * dec + jnp.where(bd, cs, 0.0)

    order = list(order)
    ahead = min(GLA_LOOKAHEAD, nchunk)
    pending = [intra(ci) for ci in order[:ahead]]
    prev = None
    for n, ci in enumerate(order):
        if n + ahead < nchunk:
            pending.append(intra(order[n + ahead]))
        o, S = combine(ci, S, *pending.pop(0))
        if prev is not None:
            emit(*prev)
        prev = (ci, o)
    emit(*prev)
    return S


def _head_mean_matrix():
    hr = lax.broadcasted_iota(jnp.int32, (GLA_WIDTH, GLA_WIDTH), 0) // GLA_DV
    hc = lax.broadcasted_iota(jnp.int32, (GLA_WIDTH, GLA_WIDTH), 1) // GLA_DV
    return jnp.where(hr == hc, 1.0 / GLA_DV, 0.0).astype(BF16)


def _gla_norm_gate(ot, head_mean, nw, gate):
    sh, sl = _split_bf16(ot * ot)
    msq = _dot(sh, head_mean) + _dot(sl, head_mean)
    return (ot * lax.rsqrt(msq + NORM_EPS) * nw * gate).astype(BF16)


def _gla_kernel(*refs, reverse, tg, nt):
    if reverse:
        gl_ref, v_ref, dec_ref, s0_ref, of_ref, gc_ref, nw_ref, o_ref, sfin_ref, st_ref = refs
        head_mean = _head_mean_matrix()
    else:
        gl_ref, v_ref, dec_ref, s0_ref, o_ref, sfin_ref, st_ref = refs
    t = pl.program_id(1)
    C = GLA_CHUNK

    @pl.when(t == 0)
    def _():
        st_ref[...] = s0_ref[0]

    def emit(ci, o):
        r0 = ci * C
        if reverse:
            ot = o + of_ref[0, r0:r0 + C, :]
            o_ref[0, r0:r0 + C, :] = _gla_norm_gate(ot, head_mean, nw_ref[...], gc_ref[0, r0:r0 + C, :])
        else:
            o_ref[0, r0:r0 + C, :] = o

    S = _gla_scan(gl_ref, v_ref, dec_ref, st_ref[...], reverse, tg, emit)
    st_ref[...] = S

    @pl.when(t == nt - 1)
    def _():
        sfin_ref[0] = S


def _gla_pass(gl, v, dec, s0, reverse, tg, extra=None):
    B, S, _ = gl.shape
    nt = S // tg
    nch = tg // GLA_CHUNK
    if reverse:
        tmap = lambda b, t: (b, nt - 1 - t, 0)
    else:
        tmap = lambda b, t: (b, t, 0)
    tok = lambda w: pl.BlockSpec((1, tg, w), tmap)
    st_spec = pl.BlockSpec((1, GLA_WIDTH, GLA_KW), lambda b, t: (b, 0, 0))
    in_specs = [tok(384), tok(256), pl.BlockSpec((1, nch, 2 * GLA_KW), tmap), st_spec]
    args = [gl, v, dec, s0]
    if reverse:
        of, gc, nw = extra
        in_specs += [tok(256), tok(256), pl.BlockSpec((1, GLA_WIDTH), lambda b, t: (0, 0))]
        args += [of, gc, nw.reshape(1, GLA_WIDTH)]
    return pl.pallas_call(
        functools.partial(_gla_kernel, reverse=reverse, tg=tg, nt=nt),
        grid=(B, nt),
        in_specs=in_specs,
        out_specs=[tok(256), st_spec],
        out_shape=[jax.ShapeDtypeStruct((B, S, GLA_WIDTH), BF16 if reverse else F32),
                   jax.ShapeDtypeStruct((B, GLA_WIDTH, GLA_KW), F32)],
        scratch_shapes=[pltpu.VMEM((GLA_WIDTH, GLA_KW), F32)],
        compiler_params=_cparams(("parallel", "arbitrary"), 32),
        name="gla_bwd" if reverse else "gla_fwd",
    )(*args)


def _tail_kernel(gl_ref, v_ref, dec_ref, s0_ref, of_ref, gc_ref, nw_ref, a_ref, b_ref, x_ref, mod_ref, w_ref, fw_ref,
                 o_ref, sfin_ref, st_ref, *, ts, nt, final):
    t = pl.program_id(1)
    C = GLA_CHUNK
    nchunk = ts // C
    part = max(nchunk // TAIL_PARTS, 1)

    @pl.when(t == 0)
    def _():
        st_ref[...] = s0_ref[0]

    yab = _dot(a_ref[0], w_ref[0:256, :]) + _dot(b_ref[0], w_ref[256:768, :])
    head_mean = _head_mean_matrix()
    gate = mod_ref[:, 2 * D_MODEL:3 * D_MODEL]
    c_blk = {}

    def emit(ci, o):
        r0 = ci * C
        ot = o + of_ref[0, r0:r0 + C, :]
        c_blk[ci] = _gla_norm_gate(ot, head_mean, nw_ref[...], gc_ref[0, r0:r0 + C, :])
        if ci % part == 0:
            grp = range(ci, ci + part)
            rows = slice(ci * C, (ci + part) * C)
            y = yab[rows] + _dot(jnp.concatenate([c_blk[i] for i in grp], axis=0), w_ref[768:1024, :])
            xn = x_ref[0, rows, :] + gate * y
            if final:
                ms = jnp.mean(xn * xn, axis=-1, keepdims=True)
                xn = xn * lax.rsqrt(ms + NORM_EPS) * fw_ref[...]
            o_ref[0, rows, :] = xn

    S = _gla_scan(gl_ref, v_ref, dec_ref, st_ref[...], True, ts, emit)
    st_ref[...] = S

    @pl.when(t == nt - 1)
    def _():
        sfin_ref[0] = S


def _layer_tail(glb, gv, dec, s0b, of, gc, norm_w, a, b, x, mod4, layer, mod_row, w_out, final_w, ts, final):
    B, S, _ = x.shape
    nt = S // ts
    nch = ts // GLA_CHUNK
    tile = lambda t: nt - 1 - t
    if mod_row is None:
        mod_map = lambda bb, t: (layer, bb, 0, 0)
    else:
        mod_map = lambda bb, t: (layer, mod_row, 0, 0)
    tok = lambda w: pl.BlockSpec((1, ts, w), lambda bb, t: (bb, tile(t), 0))
    st_spec = pl.BlockSpec((1, GLA_WIDTH, GLA_KW), lambda bb, t: (bb, 0, 0))
    row = lambda a: a.reshape(1, -1)
    vec = lambda w: pl.BlockSpec((1, w), lambda bb, t: (0, 0))
    return pl.pallas_call(
        functools.partial(_tail_kernel, ts=ts, nt=nt, final=final),
        grid=(B, nt),
        in_specs=[tok(384), tok(256), pl.BlockSpec((1, nch, 2 * GLA_KW), lambda bb, t: (bb, tile(t), 0)), st_spec,
                  tok(256), tok(256), vec(GLA_WIDTH),
                  tok(CONV_WIDTH), tok(ATTN_WIDTH), tok(D_MODEL),
                  pl.BlockSpec((None, None, 1, 3 * D_MODEL), mod_map),
                  pl.BlockSpec((D_MODEL, D_MODEL), lambda bb, t: (0, 0)),
                  vec(D_MODEL)],
        out_specs=[tok(D_MODEL), st_spec],
        out_shape=[jax.ShapeDtypeStruct((B, S, D_MODEL), F32),
                   jax.ShapeDtypeStruct((B, GLA_WIDTH, GLA_KW), F32)],
        scratch_shapes=[pltpu.VMEM((GLA_WIDTH, GLA_KW), F32)],
        compiler_params=_cparams(("parallel", "arbitrary"), 48),
        name="layer_tail",
    )(glb, gv, dec, s0b, of, gc, row(norm_w), a, b, x, mod4, w_out, row(final_w))


def _rope_tables(n_tokens):
    rows = n_tokens // GRID_W
    r = jnp.repeat(jnp.arange(rows, dtype=F32), GRID_W)
    col = jnp.tile(jnp.arange(GRID_W, dtype=F32), rows)
    n_freq = ATTN_HEAD_DIM // 4
    inv = ROPE_THETA ** (-jnp.arange(n_freq, dtype=F32) / n_freq)
    ang = jnp.concatenate([r[:, None] * inv, col[:, None] * inv], axis=-1)
    cos = jnp.tile(jnp.cos(ang), (1, 4))
    sin = jnp.sin(ang)
    return cos, jnp.concatenate([-sin, -sin, sin, sin], axis=-1)


def _split_heads(w, n_pairs):
    lead = w.shape[:-1]
    w = w.reshape(lead + (n_pairs, 2, 2, 32))
    return jnp.swapaxes(w, -2, -3).reshape(lead + (n_pairs * LANES,))


def _relayout_w_in(w_in):
    pad = jnp.zeros(w_in.shape[:-1] + (N_IN_PAD - N_IN,), F32)
    return jnp.concatenate([
        w_in[..., 0:768],
        _split_heads(w_in[..., 768:1280], 4),
        _split_heads(w_in[..., 1280:1408], 1),
        w_in[..., 1408:2560],
        w_in[..., 2592:2848],
        w_in[..., 2560:2592],
        pad], axis=-1).astype(BF16)


def _cumsum_matrices():
    n = GLA_GROUP // GLA_CHUNK
    low = np.kron(np.eye(n), np.tril(np.ones((GLA_CHUNK, GLA_CHUNK))))
    return jnp.asarray(np.stack([low, low.T]), BF16)


def kernel(x, c, ctx, c_ctx, w_mod, b_mod, w_in, conv_w, conv_b, conv_ln_w, conv_ln_b, attn_sink,
           gla_w_up, gla_b_up, gla_norm_w, w_out, final_norm_w):
    B, S, D = x.shape
    NC = ctx.shape[1]
    TS = 512

    w_p = _relayout_w_in(w_in)
    w_o = w_out.astype(BF16)
    w_up2 = jnp.zeros((DEPTH, LANES, 256), F32)
    w_up2 = w_up2.at[:, 0:GLA_LOW_RANK, 0:128].set(gla_w_up[:, 0])
    w_up2 = w_up2.at[:, GLA_LOW_RANK:2 * GLA_LOW_RANK, 128:256].set(gla_w_up[:, 1])
    wuh = w_up2.astype(BF16)
    wul = (w_up2 - wuh.astype(F32)).astype(BF16)
    bup = gla_b_up.reshape(DEPTH, 1, 256)
    tri = _cumsum_matrices()
    cos_l, sin_l = _rope_tables(S)
    cos_c = jnp.ones((NC, LANES), F32)
    sin_c = jnp.zeros((NC, LANES), F32)
    zero_state = jnp.zeros((B, GLA_WIDTH, GLA_KW), F32)

    cc = jnp.concatenate([c, c_ctx[None, :], jnp.zeros((16 - B - 1, D), F32)], axis=0)
    mod4 = _modulation(cc, w_mod, b_mod).reshape(DEPTH, 16, 1, 3 * D)
    CTX_ROW = B

    xl, xc = x, ctx
    for i in range(DEPTH):
        last = i == DEPTH - 1
        conv_p = (conv_w[i], conv_b[i], conv_ln_w[i], conv_ln_b[i])
        (a_c, q_c, kk_c, vt_c, gb_c, gc_c, glf_c, glb_c, gv_c, dec_c) = _project(
            xc, mod4, i, CTX_ROW, cos_c, sin_c, w_p[i], wuh[i], wul[i], bup[i], tri, *conv_p, NC)
        of_c, s_f = _gla_pass(glf_c, gv_c, dec_c, zero_state, False, NC)
        if last:
            _, s_b = _gla_pass(glb_c, gv_c, dec_c, zero_state, True, NC, extra=(of_c, gc_c, gla_norm_w[i]))
        else:
            b_c = _attention(q_c, kk_c, vt_c, kk_c, vt_c, gb_c, attn_sink[i], NC, False)
            xc, s_b = _layer_tail(glb_c, gv_c, dec_c, zero_state, of_c, gc_c, gla_norm_w[i], a_c, b_c, xc,
                                  mod4, i, CTX_ROW, w_o[i], final_norm_w, NC, False)
        (a_l, q_l, kk_l, vt_l, gb_l, gc_l, glf_l, glb_l, gv_l, dec_l) = _project(
            xl, mod4, i, None, cos_l, sin_l, w_p[i], wuh[i], wul[i], bup[i], tri, *conv_p, TS)
        b_l = _attention(q_l, kk_l, vt_l, kk_c, vt_c, gb_l, attn_sink[i], TS, True)
        of_l, _ = _gla_pass(glf_l, gv_l, dec_l, s_f, False, TS)
        xl, _ = _layer_tail(glb_l, gv_l, dec_l, s_b, of_l, gc_l, gla_norm_w[i], a_l, b_l, xl,
                            mod4, i, None, w_o[i], final_norm_w, TS, last)
    return xl
```

```python
import functools
import math

import numpy as np
import jax
import jax.numpy as jnp
from jax import lax
from jax.experimental import pallas as pl
from jax.experimental.pallas import tpu as pltpu

F32 = jnp.float32
BF16 = jnp.bfloat16

D_MODEL = 1024
DEPTH = 2
GRID_W = 64
CONV_WIDTH = 256
CONV_KERNEL = 31
CONV_HALO = 16
ATTN_WIDTH = 512
ATTN_HEAD_DIM = 64
ATTN_HEADS = 8
ATTN_KV_HEADS = 2
ATTN_BLOCK = 128
ATTN_LOOKAHEAD = 2
ROPE_THETA = 10000.0
GLA_WIDTH = 256
GLA_HEADS = 4
GLA_DV = 64
GLA_DK = 32
GLA_KW = GLA_HEADS * GLA_DK
GLA_LOW_RANK = 16
GLA_TAU = 16.0
GLA_CHUNK = 64
GLA_LOOKAHEAD = 2
TAIL_PARTS = 2
GLA_GROUP = 256
NORM_EPS = 1e-6
NEG_INF = -1e30
PASS_CAP = 3.0e38
LOG2E = math.log2(math.e)
LANES = 128
SUBLANES = 8
TOKEN_TILE = 512

N_IN = 2848
N_IN_PAD = 2944

P_A = 0
P_Q = 768
P_KV = 1280
P_BG = 1536
P_C = 2048
P_CG = 2560
P_LR = 2816


def _sigmoid(x):
    return 1.0 / (1.0 + jnp.exp(-x))


def _silu(x):
    return x * _sigmoid(x)


def _split_bf16(x):
    hi = x.astype(BF16)
    lo = (x - hi.astype(F32)).astype(BF16)
    return hi, lo


def _dot(a, b):
    return jnp.dot(a, b, preferred_element_type=F32)


def _dot_nt(a, b):
    return lax.dot_general(a, b, (((1,), (1,)), ((), ())), preferred_element_type=F32)


def _dot_tn(a, b):
    return lax.dot_general(a, b, (((0,), (0,)), ((), ())), preferred_element_type=F32)


def _cparams(sem, vmem_mb):
    return pltpu.CompilerParams(dimension_semantics=sem, vmem_limit_bytes=vmem_mb * 1024 * 1024)


def _mod_kernel(cc_ref, w_ref, b_ref, o_ref):
    s = _silu(cc_ref[...])
    sh, sl = _split_bf16(s)
    wh, wl = _split_bf16(w_ref[0])
    o_ref[0] = _dot(sh, wh) + _dot(sl, wh) + _dot(sh, wl) + b_ref[0]


def _modulation(cc, w_mod, b_mod):
    nb = 3
    blk = 3 * D_MODEL // nb
    return pl.pallas_call(
        _mod_kernel,
        grid=(DEPTH, nb),
        in_specs=[pl.BlockSpec((16, D_MODEL), lambda l, j: (0, 0)),
                  pl.BlockSpec((1, D_MODEL, blk), lambda l, j: (l, 0, j)),
                  pl.BlockSpec((1, 1, blk), lambda l, j: (l, 0, j))],
        out_specs=pl.BlockSpec((1, 16, blk), lambda l, j: (l, 0, j)),
        out_shape=jax.ShapeDtypeStruct((DEPTH, 16, 3 * D_MODEL), F32),
        compiler_params=_cparams(("arbitrary", "arbitrary"), 32),
        name="modulation",
    )(cc, w_mod, b_mod.reshape(DEPTH, 1, 3 * D_MODEL))


def _proj_kernel(x_ref, mod_ref, cos_ref, sin_ref, w_ref, wuh_ref, wul_ref, bup_ref, tri_ref, sel_ref,
                 u_ref, q_ref, kk_ref, vt_ref, ga_ref, gb_ref, gc_ref, glf_ref, glb_ref, gv_ref, dec_ref):
    ts = x_ref.shape[1]
    x = x_ref[0]
    mod = mod_ref[...]
    shift = mod[:, 0:D_MODEL]
    scale = mod[:, D_MODEL:2 * D_MODEL]
    ms = jnp.mean(x * x, axis=-1, keepdims=True)
    h = x * lax.rsqrt(ms + NORM_EPS) * (1.0 + scale) + shift
    hb = h.astype(BF16)

    def proj(a, b):
        return _dot(hb, w_ref[:, a:b])

    zlr = proj(P_LR, P_LR + LANES)
    zh, zl = _split_bf16(zlr)

    zc = proj(P_C, P_C + 512)
    gq = zc[:, 0:128] * (GLA_DK ** -0.5)
    gk = zc[:, 128:256]
    gv_ref[0] = zc[:, 256:512].astype(BF16)

    za = proj(P_A, P_A + 768)
    u_ref[0] = za[:, 0:256] * _sigmoid(za[:, 256:512])
    ga_ref[0] = _silu(za[:, 512:768]).astype(BF16)

    wuh = wuh_ref[...]
    zup = _dot(zh, wuh) + _dot(zl, wuh) + _dot(zh, wul_ref[...]) + bup_ref[...]
    la = (jnp.minimum(zup, 0.0) - jnp.log(1.0 + jnp.exp(-jnp.abs(zup)))) * (1.0 / GLA_TAU)
    lah, lal = _split_bf16(la)

    cos = cos_ref[...]
    sin = sin_ref[...]

    def rope(t):
        return t * cos + pltpu.roll(t, 64, 1) * sin

    zq = proj(P_Q, P_Q + 512)
    for c in range(4):
        r = rope(zq[:, c * LANES:(c + 1) * LANES]) * (ATTN_HEAD_DIM ** -0.5 * LOG2E)
        q_ref[0, :, c * LANES:(c + 1) * LANES] = r.astype(BF16)

    zkv = proj(P_KV, P_KV + 256)
    lane = lax.broadcasted_iota(jnp.int32, (1, LANES), 1)
    even = (lane % 64) < 32
    k = rope(zkv[:, 0:LANES])
    kk_ref[0, :, 0:128] = jnp.where(even, k, pltpu.roll(k, 32, 1)).astype(BF16)
    kk_ref[0, :, 128:256] = jnp.where(even, pltpu.roll(k, 96, 1), k).astype(BF16)
    vt_ref[0] = zkv[:, LANES:2 * LANES].T.astype(BF16)

    nch = ts // GLA_CHUNK
    sel = sel_ref[...]
    for d, out_ref in enumerate((glf_ref, glb_ref)):
        tri = tri_ref[d]
        cols = slice(d * GLA_KW, (d + 1) * GLA_KW)
        b = jnp.concatenate(
            [_dot(tri, lah[r:r + GLA_GROUP, cols]) + _dot(tri, lal[r:r + GLA_GROUP, cols])
             for r in range(0, ts, GLA_GROUP)], axis=0)
        tot = _dot(sel, lah[:, cols]) + _dot(sel, lal[:, cols])
        b3 = b.reshape(nch, GLA_CHUNK, GLA_KW)
        bt = b3[:, 0:1, :] if d else b3[:, GLA_CHUNK - 1:GLA_CHUNK, :]
        enb = jnp.exp(-b3)
        q_in = gq.reshape(nch, GLA_CHUNK, GLA_KW) * jnp.exp(b3)
        gk3 = gk.reshape(nch, GLA_CHUNK, GLA_KW)
        k_in = gk3 * enb
        k_st = gk3 * jnp.exp(bt - b3)
        out_ref[0, :, 0:128] = q_in.reshape(ts, GLA_KW).astype(BF16)
        out_ref[0, :, 128:256] = k_in.reshape(ts, GLA_KW).astype(BF16)
        out_ref[0, :, 256:384] = k_st.reshape(ts, GLA_KW).astype(BF16)
        dec_ref[0, :, cols] = jnp.exp(tot)

    gb_ref[0] = _silu(proj(P_BG, P_BG + 512)).astype(BF16)
    gc_ref[0] = _silu(proj(P_CG, P_CG + 256)).astype(BF16)


def _project(x, mod4, layer, mod_row, cos, sin, w_p, wuh, wul, bup, tri, ts):
    B, S, _ = x.shape
    nt = S // ts
    nch = ts // GLA_CHUNK
    if mod_row is None:
        mod_map = lambda b, t: (layer, b, 0, 0)
    else:
        mod_map = lambda b, t: (layer, mod_row, 0, 0)
    sel = jnp.asarray(np.kron(np.eye(nch), np.ones((1, GLA_CHUNK))), BF16)
    tok = lambda w: pl.BlockSpec((1, ts, w), lambda b, t: (b, t, 0))
    full = lambda a: pl.BlockSpec(a.shape, lambda b, t: (0,) * a.ndim)
    tok_out = ((256, F32), (512, BF16), (256, BF16), None, (256, BF16), (512, BF16), (256, BF16),
               (384, BF16), (384, BF16), (256, BF16), None)
    out_specs, out_shape = [], []
    for i, wd in enumerate(tok_out):
        if i == 3:
            out_specs.append(pl.BlockSpec((1, LANES, ts), lambda b, t: (b, 0, t)))
            out_shape.append(jax.ShapeDtypeStruct((B, LANES, S), BF16))
        elif i == 10:
            out_specs.append(pl.BlockSpec((1, nch, 2 * GLA_KW), lambda b, t: (b, t, 0)))
            out_shape.append(jax.ShapeDtypeStruct((B, S // GLA_CHUNK, 2 * GLA_KW), F32))
        else:
            out_specs.append(tok(wd[0]))
            out_shape.append(jax.ShapeDtypeStruct((B, S, wd[0]), wd[1]))
    return pl.pallas_call(
        _proj_kernel,
        grid=(B, nt),
        in_specs=[tok(D_MODEL),
                  pl.BlockSpec((None, None, 1, 3 * D_MODEL), mod_map),
                  pl.BlockSpec((ts, LANES), lambda b, t: (t, 0)),
                  pl.BlockSpec((ts, LANES), lambda b, t: (t, 0)),
                  full(w_p), full(wuh), full(wul), full(bup), full(tri), full(sel)],
        out_specs=out_specs,
        out_shape=out_shape,
        compiler_params=_cparams(("parallel", "arbitrary"), 56),
        name="in_proj",
    )(x, mod4, cos, sin, w_p, wuh, wul, bup, tri, sel)


def _conv_prepare(is_first, is_last, um_ref, up_ref, un_ref, buf_ref, sh_ref, tc):
    H = CONV_HALO
    buf_ref[0:H] = jnp.where(is_first, 0.0, up_ref[0])
    buf_ref[H:H + tc] = um_ref[0]
    buf_ref[H + tc:2 * H + tc] = jnp.where(is_last, 0.0, un_ref[0])
    n_sh = sh_ref.shape[1]
    for p in range(SUBLANES):
        sh_ref[p] = buf_ref[p:p + n_sh, :]


def _conv_rows(r, rows, sh_ref, ga_ref, w_ref, b_ref, lnw_ref, lnb_ref):
    off = CONV_HALO - CONV_KERNEL // 2
    acc = jnp.zeros((rows, CONV_WIDTH), F32)
    for k in range(CONV_KERNEL):
        s = off + k
        a0 = r + (s // SUBLANES) * SUBLANES
        acc = acc + w_ref[k:k + 1, :] * sh_ref[s % SUBLANES, a0:a0 + rows, :]
    acc = acc + b_ref[...]
    mu = jnp.mean(acc, axis=-1, keepdims=True)
    cen = acc - mu
    var = jnp.mean(cen * cen, axis=-1, keepdims=True)
    y = cen * lax.rsqrt(var + NORM_EPS) * lnw_ref[...] + lnb_ref[...]
    return (_silu(y) * ga_ref[0, r:r + rows, :]).astype(BF16)


def _gla_scan_steps(gl_ref, v_ref, dec_ref, S, reverse, tg, emit, final_state):
    C = GLA_CHUNK
    lane_k = lax.broadcasted_iota(jnp.int32, (1, GLA_KW), 1)
    lane_v = lax.broadcasted_iota(jnp.int32, (1, GLA_WIDTH), 1)
    hm = [jnp.where(lane_k // GLA_DK == h, 1.0, 0.0).astype(BF16) for h in range(GLA_HEADS)]
    vm = [jnp.where(lane_v // GLA_DV == h, 1.0, 0.0).astype(BF16) for h in range(GLA_HEADS)]
    rw = lax.broadcasted_iota(jnp.int32, (C, GLA_WIDTH), 0)
    sw = lax.broadcasted_iota(jnp.int32, (C, GLA_WIDTH), 1) % C
    cmask = (sw >= rw) if reverse else (sw <= rw)
    bd = (lax.broadcasted_iota(jnp.int32, (GLA_WIDTH, GLA_KW), 0) // GLA_DV
          == lax.broadcasted_iota(jnp.int32, (GLA_WIDTH, GLA_KW), 1) // GLA_DK)

    nchunk = tg // C
    order = list(range(nchunk - 1, -1, -1) if reverse else range(nchunk))
    d_off = GLA_KW if reverse else 0

    def intra(ci):
        r0 = ci * C
        q_in = gl_ref[0, r0:r0 + C, 0:128]
        k_in = gl_ref[0, r0:r0 + C, 128:256]
        k_st = gl_ref[0, r0:r0 + C, 256:384]
        v = v_ref[0, r0:r0 + C, :]
        kbd = jnp.concatenate([k_in * hm[h] for h in range(GLA_HEADS)], axis=0)
        return q_in, v, _dot_nt(q_in, kbd), _dot_tn(v, k_st)

    def combine(ci, S, q_in, v, att, cs):
        dec = dec_ref[0, ci:ci + 1, d_off:d_off + GLA_KW]
        att = jnp.where(cmask, att, 0.0).astype(BF16)
        vbd = jnp.concatenate([v * vm[h] for h in range(GLA_HEADS)], axis=0)
        o = _dot(att, vbd) + _dot_nt(q_in, S.astype(BF16))
        return o, S * dec + jnp.where(bd, cs, 0.0)

    ahead = min(GLA_LOOKAHEAD, nchunk)
    pending = [intra(ci) for ci in order[:ahead]]
    prev = None
    for n, ci in enumerate(order):
        if n + ahead < nchunk:
            pending.append(intra(order[n + ahead]))
        o, S = combine(ci, S, *pending.pop(0))
        if prev is not None:
            emit(*prev)
        prev = (ci, o)
        if n + 1 < nchunk:
            yield
    emit(*prev)
    final_state(S)


def _gla_scan(gl_ref, v_ref, dec_ref, S, reverse, tg, emit):
    out = []
    for _ in _gla_scan_steps(gl_ref, v_ref, dec_ref, S, reverse, tg, emit, out.append):
        pass
    return out[0]


def _head_mean_matrix():
    hr = lax.broadcasted_iota(jnp.int32, (GLA_WIDTH, GLA_WIDTH), 0) // GLA_DV
    hc = lax.broadcasted_iota(jnp.int32, (GLA_WIDTH, GLA_WIDTH), 1) // GLA_DV
    return jnp.where(hr == hc, 1.0 / GLA_DV, 0.0).astype(BF16)


def _gla_norm_gate(ot, head_mean, nw, gate):
    sh, sl = _split_bf16(ot * ot)
    msq = _dot(sh, head_mean) + _dot(sl, head_mean)
    return (ot * lax.rsqrt(msq + NORM_EPS) * nw * gate).astype(BF16)


def _attn_kernel(*refs, nsub, nblk, nt, has_local):
    (sink_ref, q_ref, km_ref, kp_ref, kn_ref, vm_ref, vp_ref, vn_ref, ck_ref, cv_ref, cap_ref, gb_ref) = refs[:12]
    if has_local:
        gl_ref, gv_ref, dec_ref, s0_ref, o_ref, of_ref, sfin_ref, kwin_ref, vwin_ref, st_ref = refs[12:]
    else:
        o_ref, kwin_ref, vwin_ref = refs[12:]
    t = pl.program_id(1)
    A = ATTN_BLOCK
    G = ATTN_HEADS // ATTN_KV_HEADS
    lane = lax.broadcasted_iota(jnp.int32, (1, LANES), 1)
    m_even = jnp.where((lane % 64) < 32, 1.0, 0.0).astype(BF16)
    m_odd = jnp.where((lane % 64) < 32, 0.0, 1.0).astype(BF16)
    strip = lax.broadcasted_iota(jnp.int32, (1, G * A), 1) // A
    if has_local:
        kwin_ref[0:A] = kp_ref[0]
        kwin_ref[A:A + nsub * A] = km_ref[0]
        kwin_ref[A + nsub * A:2 * A + nsub * A] = kn_ref[0]
        vwin_ref[0] = vp_ref[0]
        for s in range(nsub):
            vwin_ref[1 + s] = vm_ref[0, :, s * A:(s + 1) * A]
        vwin_ref[nsub + 1] = vn_ref[0]

        @pl.when(t == 0)
        def _():
            st_ref[...] = s0_ref[0]

    def scores(j, g):
        r0 = j * A
        q0 = q_ref[0, r0:r0 + A, (2 * g) * LANES:(2 * g + 1) * LANES]
        q1 = q_ref[0, r0:r0 + A, (2 * g + 1) * LANES:(2 * g + 2) * LANES]
        qs = jnp.concatenate([q0 * m_even, q0 * m_odd, q1 * m_even, q1 * m_odd], axis=0)
        sc = _dot_nt(ck_ref[0, :, g * LANES:(g + 1) * LANES], qs)
        sl = None
        if has_local:
            sl = _dot_nt(kwin_ref[r0:r0 + 3 * A, g * LANES:(g + 1) * LANES], qs)
        return sc, sl

    def finish(j, g, sc, sl):
        r0 = j * A
        sink = jnp.zeros((1, G * A), F32)
        for i in range(G):
            sink = jnp.where(strip == i, sink_ref[G * g + i] * LOG2E, sink)
        m = jnp.maximum(jnp.max(sc, axis=0, keepdims=True), sink)
        if has_local:
            blk = t * nsub + j
            cap = cap_ref[jnp.where(blk == 0, 1, 0) + jnp.where(blk == nblk - 1, 2, 0)]
            sl = jnp.minimum(sl, jnp.concatenate([cap] * G, axis=1))
            m = jnp.maximum(m, jnp.max(sl, axis=0, keepdims=True))
        vc = cv_ref[0, g * 64:(g + 1) * 64, :]
        ones = jnp.ones((16, vc.shape[1]), BF16)
        ot = _dot(jnp.concatenate([vc, ones], axis=0), jnp.exp2(sc - m).astype(BF16))
        if has_local:
            vl = jnp.concatenate([vwin_ref[j + i, g * 64:(g + 1) * 64, :] for i in range(3)], axis=1)
            ones = jnp.ones((16, 3 * A), BF16)
            ot = ot + _dot(jnp.concatenate([vl, ones], axis=0), jnp.exp2(sl - m).astype(BF16))
        den = ot[64:65, :] + jnp.exp2(sink - m)
        o = ot[0:64, :] * (1.0 / den)
        for i in range(2):
            c = 2 * g + i
            pair = jnp.concatenate([o[:, (2 * i) * A:(2 * i + 1) * A], o[:, (2 * i + 1) * A:(2 * i + 2) * A]],
                                   axis=0)
            gate = gb_ref[0, r0:r0 + A, c * LANES:(c + 1) * LANES]
            o_ref[0, r0:r0 + A, c * LANES:(c + 1) * LANES] = (pair.T * gate).astype(BF16)

    if has_local:
        def emit(ci, o):
            of_ref[0, ci * GLA_CHUNK:(ci + 1) * GLA_CHUNK, :] = o

        def final_state(S):
            st_ref[...] = S

            @pl.when(t == nt - 1)
            def _():
                sfin_ref[0] = S

        scan = _gla_scan_steps(gl_ref, gv_ref, dec_ref, st_ref[...], False, nsub * A, emit, final_state)
    else:
        scan = iter(())

    units = [(j, g) for j in range(nsub) for g in range(ATTN_KV_HEADS)]
    ahead = min(ATTN_LOOKAHEAD, len(units))
    pending = [scores(*u) for u in units[:ahead]]
    for n, unit in enumerate(units):
        if n + ahead < len(units):
            pending.append(scores(*units[n + ahead]))
        next(scan, None)
        finish(*unit, *pending.pop(0))
    for _ in scan:
        pass


def _attn_caps():
    A = ATTN_BLOCK
    c = np.arange(3 * A)[:, None]
    q = np.arange(A)[None, :]
    band = (c >= q) & (c <= q + 2 * A)
    caps = [band, band & (c >= A), band & (c < 2 * A)]
    return jnp.asarray(np.stack([np.where(b, PASS_CAP, NEG_INF) for b in caps]), F32)


def _attention(q, kk, vt, ckk, cvt, gb, sink, tq, gla=None):
    B, S, _ = q.shape
    nt = S // tq
    nsub = tq // ATTN_BLOCK
    nblk = S // ATTN_BLOCK
    nch = tq // GLA_CHUNK
    A = ATTN_BLOCK
    nc = ckk.shape[1]
    has_local = gla is not None
    assert not has_local or nblk >= 2
    caps = _attn_caps()
    prev = lambda t: jnp.maximum(t * nsub - 1, 0)
    nxt = lambda t: jnp.minimum((t + 1) * nsub, nblk - 1)
    tok = lambda w: pl.BlockSpec((1, tq, w), lambda b, t: (b, t, 0))
    st_spec = pl.BlockSpec((1, GLA_WIDTH, GLA_KW), lambda b, t: (b, 0, 0))
    in_specs = [pl.BlockSpec(memory_space=pltpu.SMEM),
                tok(ATTN_WIDTH), tok(256),
                pl.BlockSpec((1, A, 256), lambda b, t: (b, prev(t), 0)),
                pl.BlockSpec((1, A, 256), lambda b, t: (b, nxt(t), 0)),
                pl.BlockSpec((1, LANES, tq), lambda b, t: (b, 0, t)),
                pl.BlockSpec((1, LANES, A), lambda b, t: (b, 0, prev(t))),
                pl.BlockSpec((1, LANES, A), lambda b, t: (b, 0, nxt(t))),
                pl.BlockSpec((1, nc, 256), lambda b, t: (b, 0, 0)),
                pl.BlockSpec((1, LANES, nc), lambda b, t: (b, 0, 0)),
                pl.BlockSpec(caps.shape, lambda b, t: (0, 0, 0)),
                tok(ATTN_WIDTH)]
    args = [sink, q, kk, kk, kk, vt, vt, vt, ckk, cvt, caps, gb]
    out_specs = [tok(ATTN_WIDTH)]
    out_shape = [jax.ShapeDtypeStruct((B, S, ATTN_WIDTH), BF16)]
    scratch = [pltpu.VMEM((tq + 2 * A, 256), BF16), pltpu.VMEM((nsub + 2, LANES, A), BF16)]
    if has_local:
        in_specs += [tok(384), tok(256), pl.BlockSpec((1, nch, 2 * GLA_KW), lambda b, t: (b, t, 0)), st_spec]
        args += list(gla)
        out_specs += [tok(GLA_WIDTH), st_spec]
        out_shape += [jax.ShapeDtypeStruct((B, S, GLA_WIDTH), F32),
                      jax.ShapeDtypeStruct((B, GLA_WIDTH, GLA_KW), F32)]
        scratch.append(pltpu.VMEM((GLA_WIDTH, GLA_KW), F32))
    out = pl.pallas_call(
        functools.partial(_attn_kernel, nsub=nsub, nblk=nblk, nt=nt, has_local=has_local),
        grid=(B, nt),
        in_specs=in_specs,
        out_specs=out_specs,
        out_shape=out_shape,
        scratch_shapes=scratch,
        compiler_params=_cparams(("parallel", "arbitrary"), 40),
        name="attention_gla_fwd" if has_local else "ctx_attention",
    )(*args)
    return out if has_local else out[0]


def _gla_kernel(*refs, reverse, tg, nt):
    if reverse:
        gl_ref, v_ref, dec_ref, s0_ref, of_ref, gc_ref, nw_ref, o_ref, sfin_ref, st_ref = refs
        head_mean = _head_mean_matrix()
    else:
        gl_ref, v_ref, dec_ref, s0_ref, o_ref, sfin_ref, st_ref = refs
    t = pl.program_id(1)
    C = GLA_CHUNK

    @pl.when(t == 0)
    def _():
        st_ref[...] = s0_ref[0]

    def emit(ci, o):
        r0 = ci * C
        if reverse:
            ot = o + of_ref[0, r0:r0 + C, :]
            o_ref[0, r0:r0 + C, :] = _gla_norm_gate(ot, head_mean, nw_ref[...], gc_ref[0, r0:r0 + C, :])
        else:
            o_ref[0, r0:r0 + C, :] = o

    S = _gla_scan(gl_ref, v_ref, dec_ref, st_ref[...], reverse, tg, emit)
    st_ref[...] = S

    @pl.when(t == nt - 1)
    def _():
        sfin_ref[0] = S


def _gla_pass(gl, v, dec, s0, reverse, tg, extra=None):
    B, S, _ = gl.shape
    nt = S // tg
    nch = tg // GLA_CHUNK
    if reverse:
        tmap = lambda b, t: (b, nt - 1 - t, 0)
    else:
        tmap = lambda b, t: (b, t, 0)
    tok = lambda w: pl.BlockSpec((1, tg, w), tmap)
    st_spec = pl.BlockSpec((1, GLA_WIDTH, GLA_KW), lambda b, t: (b, 0, 0))
    in_specs = [tok(384), tok(256), pl.BlockSpec((1, nch, 2 * GLA_KW), tmap), st_spec]
    args = [gl, v, dec, s0]
    if reverse:
        of, gc, nw = extra
        in_specs += [tok(256), tok(256), pl.BlockSpec((1, GLA_WIDTH), lambda b, t: (0, 0))]
        args += [of, gc, nw.reshape(1, GLA_WIDTH)]
    return pl.pallas_call(
        functools.partial(_gla_kernel, reverse=reverse, tg=tg, nt=nt),
        grid=(B, nt),
        in_specs=in_specs,
        out_specs=[tok(256), st_spec],
        out_shape=[jax.ShapeDtypeStruct((B, S, GLA_WIDTH), BF16 if reverse else F32),
                   jax.ShapeDtypeStruct((B, GLA_WIDTH, GLA_KW), F32)],
        scratch_shapes=[pltpu.VMEM((GLA_WIDTH, GLA_KW), F32)],
        compiler_params=_cparams(("parallel", "arbitrary"), 32),
        name="gla_bwd" if reverse else "gla_fwd",
    )(*args)


def _tail_kernel(gl_ref, v_ref, dec_ref, s0_ref, of_ref, gc_ref, nw_ref,
                 um_ref, up_ref, un_ref, ga_ref, cw_ref, cb_ref, lnw_ref, lnb_ref,
                 b_ref, x_ref, mod_ref, w_ref, fw_ref,
                 o_ref, sfin_ref, st_ref, buf_ref, sh_ref, *, ts, nt, final):
    t = pl.program_id(1)
    C = GLA_CHUNK
    nchunk = ts // C
    part = max(nchunk // TAIL_PARTS, 1)

    @pl.when(t == 0)
    def _():
        st_ref[...] = s0_ref[0]

    _conv_prepare(t == nt - 1, t == 0, um_ref, up_ref, un_ref, buf_ref, sh_ref, ts)
    yb = _dot(b_ref[0], w_ref[256:768, :])
    head_mean = _head_mean_matrix()
    gate = mod_ref[:, 2 * D_MODEL:3 * D_MODEL]
    a_blk, c_blk = {}, {}

    def emit(ci, o):
        r0 = ci * C
        ot = o + of_ref[0, r0:r0 + C, :]
        c_blk[ci] = _gla_norm_gate(ot, head_mean, nw_ref[...], gc_ref[0, r0:r0 + C, :])
        a_blk[ci] = _conv_rows(r0, C, sh_ref, ga_ref, cw_ref, cb_ref, lnw_ref, lnb_ref)
        if ci % part == 0:
            grp = range(ci, ci + part)
            rows = slice(ci * C, (ci + part) * C)
            y = (yb[rows] + _dot(jnp.concatenate([a_blk[i] for i in grp], axis=0), w_ref[0:256, :])
                 + _dot(jnp.concatenate([c_blk[i] for i in grp], axis=0), w_ref[768:1024, :]))
            xn = x_ref[0, rows, :] + gate * y
            if final:
                ms = jnp.mean(xn * xn, axis=-1, keepdims=True)
                xn = xn * lax.rsqrt(ms + NORM_EPS) * fw_ref[...]
            o_ref[0, rows, :] = xn

    S = _gla_scan(gl_ref, v_ref, dec_ref, st_ref[...], True, ts, emit)
    st_ref[...] = S

    @pl.when(t == nt - 1)
    def _():
        sfin_ref[0] = S


def _layer_tail(glb, gv, dec, s0b, of, gc, norm_w, u, ga, conv_w, conv_b, ln_w, ln_b, b, x, mod4, layer, mod_row,
                w_out, final_w, ts, final):
    B, S, _ = x.shape
    nt = S // ts
    nch = ts // GLA_CHUNK
    hb = ts // CONV_HALO
    nhb = S // CONV_HALO
    n_sh = ts + 2 * CONV_HALO - SUBLANES
    tile = lambda t: nt - 1 - t
    if mod_row is None:
        mod_map = lambda bb, t: (layer, bb, 0, 0)
    else:
        mod_map = lambda bb, t: (layer, mod_row, 0, 0)
    tok = lambda w: pl.BlockSpec((1, ts, w), lambda bb, t: (bb, tile(t), 0))
    st_spec = pl.BlockSpec((1, GLA_WIDTH, GLA_KW), lambda bb, t: (bb, 0, 0))
    row = lambda a: a.reshape(1, -1)
    vec = lambda w: pl.BlockSpec((1, w), lambda bb, t: (0, 0))
    halo_prev = pl.BlockSpec((1, CONV_HALO, CONV_WIDTH), lambda bb, t: (bb, jnp.maximum(tile(t) * hb - 1, 0), 0))
    halo_next = pl.BlockSpec((1, CONV_HALO, CONV_WIDTH),
                             lambda bb, t: (bb, jnp.minimum((tile(t) + 1) * hb, nhb - 1), 0))
    return pl.pallas_call(
        functools.partial(_tail_kernel, ts=ts, nt=nt, final=final),
        grid=(B, nt),
        in_specs=[tok(384), tok(256), pl.BlockSpec((1, nch, 2 * GLA_KW), lambda bb, t: (bb, tile(t), 0)), st_spec,
                  tok(256), tok(256), vec(GLA_WIDTH),
                  tok(CONV_WIDTH), halo_prev, halo_next, tok(CONV_WIDTH),
                  pl.BlockSpec((CONV_KERNEL, CONV_WIDTH), lambda bb, t: (0, 0)),
                  vec(CONV_WIDTH), vec(CONV_WIDTH), vec(CONV_WIDTH),
                  tok(ATTN_WIDTH), tok(D_MODEL),
                  pl.BlockSpec((None, None, 1, 3 * D_MODEL), mod_map),
                  pl.BlockSpec((D_MODEL, D_MODEL), lambda bb, t: (0, 0)),
                  vec(D_MODEL)],
        out_specs=[tok(D_MODEL), st_spec],
        out_shape=[jax.ShapeDtypeStruct((B, S, D_MODEL), F32),
                   jax.ShapeDtypeStruct((B, GLA_WIDTH, GLA_KW), F32)],
        scratch_shapes=[pltpu.VMEM((GLA_WIDTH, GLA_KW), F32),
                        pltpu.VMEM((ts + 2 * CONV_HALO, CONV_WIDTH), F32),
                        pltpu.VMEM((SUBLANES, n_sh, CONV_WIDTH), F32)],
        compiler_params=_cparams(("parallel", "arbitrary"), 48),
        name="layer_tail",
    )(glb, gv, dec, s0b, of, gc, row(norm_w), u, u, u, ga, conv_w, row(conv_b), row(ln_w), row(ln_b),
      b, x, mod4, w_out, row(final_w))


def _rope_tables(n_tokens):
    rows = n_tokens // GRID_W
    r = jnp.repeat(jnp.arange(rows, dtype=F32), GRID_W)
    col = jnp.tile(jnp.arange(GRID_W, dtype=F32), rows)
    n_freq = ATTN_HEAD_DIM // 4
    inv = ROPE_THETA ** (-jnp.arange(n_freq, dtype=F32) / n_freq)
    ang = jnp.concatenate([r[:, None] * inv, col[:, None] * inv], axis=-1)
    cos = jnp.tile(jnp.cos(ang), (1, 4))
    sin = jnp.sin(ang)
    return cos, jnp.concatenate([-sin, -sin, sin, sin], axis=-1)


def _split_heads(w, n_pairs):
    lead = w.shape[:-1]
    w = w.reshape(lead + (n_pairs, 2, 2, 32))
    return jnp.swapaxes(w, -2, -3).reshape(lead + (n_pairs * LANES,))


def _relayout_w_in(w_in):
    pad = jnp.zeros(w_in.shape[:-1] + (N_IN_PAD - N_IN,), F32)
    return jnp.concatenate([
        w_in[..., 0:768],
        _split_heads(w_in[..., 768:1280], 4),
        _split_heads(w_in[..., 1280:1408], 1),
        w_in[..., 1408:2560],
        w_in[..., 2592:2848],
        w_in[..., 2560:2592],
        pad], axis=-1).astype(BF16)


def _cumsum_matrices():
    n = GLA_GROUP // GLA_CHUNK
    low = np.kron(np.eye(n), np.tril(np.ones((GLA_CHUNK, GLA_CHUNK))))
    return jnp.asarray(np.stack([low, low.T]), BF16)


def kernel(x, c, ctx, c_ctx, w_mod, b_mod, w_in, conv_w, conv_b, conv_ln_w, conv_ln_b, attn_sink,
           gla_w_up, gla_b_up, gla_norm_w, w_out, final_norm_w):
    B, S, D = x.shape
    NC = ctx.shape[1]
    TS = TOKEN_TILE

    w_p = _relayout_w_in(w_in)
    w_o = w_out.astype(BF16)
    w_up2 = jnp.zeros((DEPTH, LANES, 256), F32)
    w_up2 = w_up2.at[:, 0:GLA_LOW_RANK, 0:128].set(gla_w_up[:, 0])
    w_up2 = w_up2.at[:, GLA_LOW_RANK:2 * GLA_LOW_RANK, 128:256].set(gla_w_up[:, 1])
    wuh = w_up2.astype(BF16)
    wul = (w_up2 - wuh.astype(F32)).astype(BF16)
    bup = gla_b_up.reshape(DEPTH, 1, 256)
    tri = _cumsum_matrices()
    cos_l, sin_l = _rope_tables(S)
    cos_c = jnp.ones((NC, LANES), F32)
    sin_c = jnp.zeros((NC, LANES), F32)
    zero_state = jnp.zeros((B, GLA_WIDTH, GLA_KW), F32)

    cc = jnp.concatenate([c, c_ctx[None, :], jnp.zeros((16 - B - 1, D), F32)], axis=0)
    mod4 = _modulation(cc, w_mod, b_mod).reshape(DEPTH, 16, 1, 3 * D)
    CTX_ROW = B

    xl, xc = x, ctx
    for i in range(DEPTH):
        last = i == DEPTH - 1
        conv_p = (conv_w[i], conv_b[i], conv_ln_w[i], conv_ln_b[i])
        (u_c, q_c, kk_c, vt_c, ga_c, gb_c, gc_c, glf_c, glb_c, gv_c, dec_c) = _project(
            xc, mod4, i, CTX_ROW, cos_c, sin_c, w_p[i], wuh[i], wul[i], bup[i], tri, NC)
        of_c, s_f = _gla_pass(glf_c, gv_c, dec_c, zero_state, False, NC)
        if last:
            _, s_b = _gla_pass(glb_c, gv_c, dec_c, zero_state, True, NC, extra=(of_c, gc_c, gla_norm_w[i]))
        else:
            b_c = _attention(q_c, kk_c, vt_c, kk_c, vt_c, gb_c, attn_sink[i], NC)
            xc, s_b = _layer_tail(glb_c, gv_c, dec_c, zero_state, of_c, gc_c, gla_norm_w[i], u_c, ga_c, *conv_p,
                                  b_c, xc, mod4, i, CTX_ROW, w_o[i], final_norm_w, NC, False)
        (u_l, q_l, kk_l, vt_l, ga_l, gb_l, gc_l, glf_l, glb_l, gv_l, dec_l) = _project(
            xl, mod4, i, None, cos_l, sin_l, w_p[i], wuh[i], wul[i], bup[i], tri, TS)
        b_l, of_l, _ = _attention(q_l, kk_l, vt_l, kk_c, vt_c, gb_l, attn_sink[i], TS,
                                  gla=(glf_l, gv_l, dec_l, s_f))
        xl, _ = _layer_tail(glb_l, gv_l, dec_l, s_b, of_l, gc_l, gla_norm_w[i], u_l, ga_l, *conv_p,
                            b_l, xl, mod4, i, None, w_o[i], final_norm_w, TS, last)
    return xl
```

```python
import functools
import math

import numpy as np
import jax
import jax.numpy as jnp
from jax import lax
from jax.experimental import pallas as pl
from jax.experimental.pallas import tpu as pltpu

F32 = jnp.float32
BF16 = jnp.bfloat16

D_MODEL = 1024
DEPTH = 2
GRID_W = 64
CONV_WIDTH = 256
CONV_KERNEL = 31
CONV_HALO = 16
ATTN_WIDTH = 512
ATTN_HEAD_DIM = 64
ATTN_HEADS = 8
ATTN_KV_HEADS = 2
ATTN_BLOCK = 128
ATTN_LOOKAHEAD = 2
ROPE_THETA = 10000.0
GLA_WIDTH = 256
GLA_HEADS = 4
GLA_DV = 64
GLA_DK = 32
GLA_KW = GLA_HEADS * GLA_DK
GLA_LOW_RANK = 16
GLA_TAU = 16.0
GLA_CHUNK = 64
GLA_LOOKAHEAD = 2
TAIL_PARTS = 2
GLA_GROUP = 256
NORM_EPS = 1e-6
NEG_INF = -1e30
PASS_CAP = 3.0e38
LOG2E = math.log2(math.e)
LANES = 128
SUBLANES = 8
TOKEN_TILE = 1024

N_IN = 2848
N_IN_PAD = 2944

P_A = 0
P_Q = 768
P_KV = 1280
P_BG = 1536
P_C = 2048
P_CG = 2560
P_LR = 2816


def _sigmoid(x):
    return 1.0 / (1.0 + jnp.exp(-x))


def _silu(x):
    return x * _sigmoid(x)


def _split_bf16(x):
    hi = x.astype(BF16)
    lo = (x - hi.astype(F32)).astype(BF16)
    return hi, lo


def _dot(a, b):
    return jnp.dot(a, b, preferred_element_type=F32)


def _dot_nt(a, b):
    return lax.dot_general(a, b, (((1,), (1,)), ((), ())), preferred_element_type=F32)


def _dot_tn(a, b):
    return lax.dot_general(a, b, (((0,), (0,)), ((), ())), preferred_element_type=F32)


def _cparams(sem, vmem_mb):
    return pltpu.CompilerParams(dimension_semantics=sem, vmem_limit_bytes=vmem_mb * 1024 * 1024)


def _mod_kernel(cc_ref, w_ref, b_ref, o_ref):
    s = _silu(cc_ref[...])
    sh, sl = _split_bf16(s)
    wh, wl = _split_bf16(w_ref[0])
    o_ref[0] = _dot(sh, wh) + _dot(sl, wh) + _dot(sh, wl) + b_ref[0]


def _modulation(cc, w_mod, b_mod):
    nb = 3
    blk = 3 * D_MODEL // nb
    return pl.pallas_call(
        _mod_kernel,
        grid=(DEPTH, nb),
        in_specs=[pl.BlockSpec((16, D_MODEL), lambda l, j: (0, 0)),
                  pl.BlockSpec((1, D_MODEL, blk), lambda l, j: (l, 0, j)),
                  pl.BlockSpec((1, 1, blk), lambda l, j: (l, 0, j))],
        out_specs=pl.BlockSpec((1, 16, blk), lambda l, j: (l, 0, j)),
        out_shape=jax.ShapeDtypeStruct((DEPTH, 16, 3 * D_MODEL), F32),
        compiler_params=_cparams(("arbitrary", "arbitrary"), 32),
        name="modulation",
    )(cc, w_mod, b_mod.reshape(DEPTH, 1, 3 * D_MODEL))


def _proj_kernel(x_ref, mod_ref, cos_ref, sin_ref, w_ref, wuh_ref, wul_ref, bup_ref, tri_ref, sel_ref,
                 u_ref, q_ref, kk_ref, vt_ref, ga_ref, gb_ref, gc_ref, glf_ref, glb_ref, gv_ref, dec_ref):
    ts = x_ref.shape[1]
    x = x_ref[0]
    mod = mod_ref[...]
    shift = mod[:, 0:D_MODEL]
    scale = mod[:, D_MODEL:2 * D_MODEL]
    ms = jnp.mean(x * x, axis=-1, keepdims=True)
    h = x * lax.rsqrt(ms + NORM_EPS) * (1.0 + scale) + shift
    hb = h.astype(BF16)

    def proj(a, b):
        return _dot(hb, w_ref[:, a:b])

    zlr = proj(P_LR, P_LR + LANES)
    zh, zl = _split_bf16(zlr)

    zc = proj(P_C, P_C + 512)
    gq = zc[:, 0:128] * (GLA_DK ** -0.5)
    gk = zc[:, 128:256]
    gv_ref[0] = zc[:, 256:512].astype(BF16)

    za = proj(P_A, P_A + 768)
    u_ref[0] = za[:, 0:256] * _sigmoid(za[:, 256:512])
    ga_ref[0] = _silu(za[:, 512:768]).astype(BF16)

    wuh = wuh_ref[...]
    zup = _dot(zh, wuh) + _dot(zl, wuh) + _dot(zh, wul_ref[...]) + bup_ref[...]
    la = (jnp.minimum(zup, 0.0) - jnp.log(1.0 + jnp.exp(-jnp.abs(zup)))) * (1.0 / GLA_TAU)
    lah, lal = _split_bf16(la)

    cos = cos_ref[...]
    sin = sin_ref[...]

    def rope(t):
        return t * cos + pltpu.roll(t, 64, 1) * sin

    zq = proj(P_Q, P_Q + 512)
    for c in range(4):
        r = rope(zq[:, c * LANES:(c + 1) * LANES]) * (ATTN_HEAD_DIM ** -0.5 * LOG2E)
        q_ref[0, :, c * LANES:(c + 1) * LANES] = r.astype(BF16)

    zkv = proj(P_KV, P_KV + 256)
    lane = lax.broadcasted_iota(jnp.int32, (1, LANES), 1)
    even = (lane % 64) < 32
    k = rope(zkv[:, 0:LANES])
    kk_ref[0, :, 0:128] = jnp.where(even, k, pltpu.roll(k, 32, 1)).astype(BF16)
    kk_ref[0, :, 128:256] = jnp.where(even, pltpu.roll(k, 96, 1), k).astype(BF16)
    vt_ref[0] = zkv[:, LANES:2 * LANES].T.astype(BF16)

    nch = ts // GLA_CHUNK
    sel = sel_ref[...]
    for d, out_ref in enumerate((glf_ref, glb_ref)):
        tri = tri_ref[d]
        cols = slice(d * GLA_KW, (d + 1) * GLA_KW)
        b = jnp.concatenate(
            [_dot(tri, lah[r:r + GLA_GROUP, cols]) + _dot(tri, lal[r:r + GLA_GROUP, cols])
             for r in range(0, ts, GLA_GROUP)], axis=0)
        tot = _dot(sel, lah[:, cols]) + _dot(sel, lal[:, cols])
        b3 = b.reshape(nch, GLA_CHUNK, GLA_KW)
        bt = b3[:, 0:1, :] if d else b3[:, GLA_CHUNK - 1:GLA_CHUNK, :]
        enb = jnp.exp(-b3)
        q_in = gq.reshape(nch, GLA_CHUNK, GLA_KW) * jnp.exp(b3)
        gk3 = gk.reshape(nch, GLA_CHUNK, GLA_KW)
        k_in = gk3 * enb
        k_st = gk3 * jnp.exp(bt - b3)
        out_ref[0, :, 0:128] = q_in.reshape(ts, GLA_KW).astype(BF16)
        out_ref[0, :, 128:256] = k_in.reshape(ts, GLA_KW).astype(BF16)
        out_ref[0, :, 256:384] = k_st.reshape(ts, GLA_KW).astype(BF16)
        dec_ref[0, :, cols] = jnp.exp(tot)

    gb_ref[0] = _silu(proj(P_BG, P_BG + 512)).astype(BF16)
    gc_ref[0] = _silu(proj(P_CG, P_CG + 256)).astype(BF16)


def _project(x, mod4, layer, mod_row, cos, sin, w_p, wuh, wul, bup, tri, ts):
    B, S, _ = x.shape
    nt = S // ts
    nch = ts // GLA_CHUNK
    if mod_row is None:
        mod_map = lambda b, t: (layer, b, 0, 0)
    else:
        mod_map = lambda b, t: (layer, mod_row, 0, 0)
    sel = jnp.asarray(np.kron(np.eye(nch), np.ones((1, GLA_CHUNK))), BF16)
    tok = lambda w: pl.BlockSpec((1, ts, w), lambda b, t: (b, t, 0))
    full = lambda a: pl.BlockSpec(a.shape, lambda b, t: (0,) * a.ndim)
    tok_out = ((256, F32), (512, BF16), (256, BF16), None, (256, BF16), (512, BF16), (256, BF16),
               (384, BF16), (384, BF16), (256, BF16), None)
    out_specs, out_shape = [], []
    for i, wd in enumerate(tok_out):
        if i == 3:
            out_specs.append(pl.BlockSpec((1, LANES, ts), lambda b, t: (b, 0, t)))
            out_shape.append(jax.ShapeDtypeStruct((B, LANES, S), BF16))
        elif i == 10:
            out_specs.append(pl.BlockSpec((1, nch, 2 * GLA_KW), lambda b, t: (b, t, 0)))
            out_shape.append(jax.ShapeDtypeStruct((B, S // GLA_CHUNK, 2 * GLA_KW), F32))
        else:
            out_specs.append(tok(wd[0]))
            out_shape.append(jax.ShapeDtypeStruct((B, S, wd[0]), wd[1]))
    return pl.pallas_call(
        _proj_kernel,
        grid=(B, nt),
        in_specs=[tok(D_MODEL),
                  pl.BlockSpec((None, None, 1, 3 * D_MODEL), mod_map),
                  pl.BlockSpec((ts, LANES), lambda b, t: (t, 0)),
                  pl.BlockSpec((ts, LANES), lambda b, t: (t, 0)),
                  full(w_p), full(wuh), full(wul), full(bup), full(tri), full(sel)],
        out_specs=out_specs,
        out_shape=out_shape,
        compiler_params=_cparams(("parallel", "arbitrary"), 56),
        name="in_proj",
    )(x, mod4, cos, sin, w_p, wuh, wul, bup, tri, sel)


def _conv_prepare(is_first, is_last, um_ref, up_ref, un_ref, buf_ref, sh_ref, tc):
    H = CONV_HALO
    buf_ref[0:H] = jnp.where(is_first, 0.0, up_ref[0])
    buf_ref[H:H + tc] = um_ref[0]
    buf_ref[H + tc:2 * H + tc] = jnp.where(is_last, 0.0, un_ref[0])
    n_sh = sh_ref.shape[1]
    for p in range(SUBLANES):
        sh_ref[p] = buf_ref[p:p + n_sh, :]


def _conv_rows(r, rows, sh_ref, ga_ref, w_ref, b_ref, lnw_ref, lnb_ref):
    off = CONV_HALO - CONV_KERNEL // 2
    acc = jnp.zeros((rows, CONV_WIDTH), F32)
    for k in range(CONV_KERNEL):
        s = off + k
        a0 = r + (s // SUBLANES) * SUBLANES
        acc = acc + w_ref[k:k + 1, :] * sh_ref[s % SUBLANES, a0:a0 + rows, :]
    acc = acc + b_ref[...]
    mu = jnp.mean(acc, axis=-1, keepdims=True)
    cen = acc - mu
    var = jnp.mean(cen * cen, axis=-1, keepdims=True)
    y = cen * lax.rsqrt(var + NORM_EPS) * lnw_ref[...] + lnb_ref[...]
    return (_silu(y) * ga_ref[0, r:r + rows, :]).astype(BF16)


def _gla_scan_steps(gl_ref, v_ref, dec_ref, S, reverse, tg, emit, final_state):
    C = GLA_CHUNK
    lane_k = lax.broadcasted_iota(jnp.int32, (1, GLA_KW), 1)
    lane_v = lax.broadcasted_iota(jnp.int32, (1, GLA_WIDTH), 1)
    hm = [jnp.where(lane_k // GLA_DK == h, 1.0, 0.0).astype(BF16) for h in range(GLA_HEADS)]
    vm = [jnp.where(lane_v // GLA_DV == h, 1.0, 0.0).astype(BF16) for h in range(GLA_HEADS)]
    rw = lax.broadcasted_iota(jnp.int32, (C, GLA_WIDTH), 0)
    sw = lax.broadcasted_iota(jnp.int32, (C, GLA_WIDTH), 1) % C
    cmask = (sw >= rw) if reverse else (sw <= rw)
    bd = (lax.broadcasted_iota(jnp.int32, (GLA_WIDTH, GLA_KW), 0) // GLA_DV
          == lax.broadcasted_iota(jnp.int32, (GLA_WIDTH, GLA_KW), 1) // GLA_DK)

    nchunk = tg // C
    order = list(range(nchunk - 1, -1, -1) if reverse else range(nchunk))
    d_off = GLA_KW if reverse else 0

    def intra(ci):
        r0 = ci * C
        q_in = gl_ref[0, r0:r0 + C, 0:128]
        k_in = gl_ref[0, r0:r0 + C, 128:256]
        k_st = gl_ref[0, r0:r0 + C, 256:384]
        v = v_ref[0, r0:r0 + C, :]
        kbd = jnp.concatenate([k_in * hm[h] for h in range(GLA_HEADS)], axis=0)
        return q_in, v, _dot_nt(q_in, kbd), _dot_tn(v, k_st)

    def combine(ci, S, q_in, v, att, cs):
        dec = dec_ref[0, ci:ci + 1, d_off:d_off + GLA_KW]
        att = jnp.where(cmask, att, 0.0).astype(BF16)
        vbd = jnp.concatenate([v * vm[h] for h in range(GLA_HEADS)], axis=0)
        o = _dot(att, vbd) + _dot_nt(q_in, S.astype(BF16))
        return o, S * dec + jnp.where(bd, cs, 0.0)

    ahead = min(GLA_LOOKAHEAD, nchunk)
    pending = [intra(ci) for ci in order[:ahead]]
    prev = None
    for n, ci in enumerate(order):
        if n + ahead < nchunk:
            pending.append(intra(order[n + ahead]))
        o, S = combine(ci, S, *pending.pop(0))
        if prev is not None:
            emit(*prev)
        prev = (ci, o)
        if n + 1 < nchunk:
            yield
    emit(*prev)
    final_state(S)


def _gla_scan(gl_ref, v_ref, dec_ref, S, reverse, tg, emit):
    out = []
    for _ in _gla_scan_steps(gl_ref, v_ref, dec_ref, S, reverse, tg, emit, out.append):
        pass
    return out[0]


def _head_mean_matrix():
    hr = lax.broadcasted_iota(jnp.int32, (GLA_WIDTH, GLA_WIDTH), 0) // GLA_DV
    hc = lax.broadcasted_iota(jnp.int32, (GLA_WIDTH, GLA_WIDTH), 1) // GLA_DV
    return jnp.where(hr == hc, 1.0 / GLA_DV, 0.0).astype(BF16)


def _gla_norm_gate(ot, head_mean, nw, gate):
    sh, sl = _split_bf16(ot * ot)
    msq = _dot(sh, head_mean) + _dot(sl, head_mean)
    return (ot * lax.rsqrt(msq + NORM_EPS) * nw * gate).astype(BF16)


def _attn_kernel(*refs, nsub, nblk, nt, has_local):
    (sink_ref, q_ref, km_ref, kp_ref, kn_ref, vm_ref, vp_ref, vn_ref, ck_ref, cv_ref, cap_ref, gb_ref) = refs[:12]
    if has_local:
        gl_ref, gv_ref, dec_ref, s0_ref, o_ref, of_ref, sfin_ref, kwin_ref, vwin_ref, st_ref = refs[12:]
    else:
        o_ref, kwin_ref, vwin_ref = refs[12:]
    t = pl.program_id(1)
    A = ATTN_BLOCK
    G = ATTN_HEADS // ATTN_KV_HEADS
    lane = lax.broadcasted_iota(jnp.int32, (1, LANES), 1)
    m_even = jnp.where((lane % 64) < 32, 1.0, 0.0).astype(BF16)
    m_odd = jnp.where((lane % 64) < 32, 0.0, 1.0).astype(BF16)
    strip = lax.broadcasted_iota(jnp.int32, (1, G * A), 1) // A
    if has_local:
        kwin_ref[0:A] = kp_ref[0]
        kwin_ref[A:A + nsub * A] = km_ref[0]
        kwin_ref[A + nsub * A:2 * A + nsub * A] = kn_ref[0]
        vwin_ref[0] = vp_ref[0]
        for s in range(nsub):
            vwin_ref[1 + s] = vm_ref[0, :, s * A:(s + 1) * A]
        vwin_ref[nsub + 1] = vn_ref[0]

        @pl.when(t == 0)
        def _():
            st_ref[...] = s0_ref[0]

    def scores(j, g):
        r0 = j * A
        q0 = q_ref[0, r0:r0 + A, (2 * g) * LANES:(2 * g + 1) * LANES]
        q1 = q_ref[0, r0:r0 + A, (2 * g + 1) * LANES:(2 * g + 2) * LANES]
        qs = jnp.concatenate([q0 * m_even, q0 * m_odd, q1 * m_even, q1 * m_odd], axis=0)
        sc = _dot_nt(ck_ref[0, :, g * LANES:(g + 1) * LANES], qs)
        sl = None
        if has_local:
            sl = _dot_nt(kwin_ref[r0:r0 + 3 * A, g * LANES:(g + 1) * LANES], qs)
        return sc, sl

    def finish(j, g, sc, sl):
        r0 = j * A
        sink = jnp.zeros((1, G * A), F32)
        for i in range(G):
            sink = jnp.where(strip == i, sink_ref[G * g + i] * LOG2E, sink)
        m = jnp.maximum(jnp.max(sc, axis=0, keepdims=True), sink)
        if has_local:
            blk = t * nsub + j
            cap = cap_ref[jnp.where(blk == 0, 1, 0) + jnp.where(blk == nblk - 1, 2, 0)]
            sl = jnp.minimum(sl, jnp.concatenate([cap] * G, axis=1))
            m = jnp.maximum(m, jnp.max(sl, axis=0, keepdims=True))
        vc = cv_ref[0, g * 64:(g + 1) * 64, :]
        ones = jnp.ones((16, vc.shape[1]), BF16)
        ot = _dot(jnp.concatenate([vc, ones], axis=0), jnp.exp2(sc - m).astype(BF16))
        if has_local:
            vl = jnp.concatenate([vwin_ref[j + i, g * 64:(g + 1) * 64, :] for i in range(3)], axis=1)
            ones = jnp.ones((16, 3 * A), BF16)
            ot = ot + _dot(jnp.concatenate([vl, ones], axis=0), jnp.exp2(sl - m).astype(BF16))
        den = ot[64:65, :] + jnp.exp2(sink - m)
        o = ot[0:64, :] * (1.0 / den)
        for i in range(2):
            c = 2 * g + i
            pair = jnp.concatenate([o[:, (2 * i) * A:(2 * i + 1) * A], o[:, (2 * i + 1) * A:(2 * i + 2) * A]],
                                   axis=0)
            gate = gb_ref[0, r0:r0 + A, c * LANES:(c + 1) * LANES]
            o_ref[0, r0:r0 + A, c * LANES:(c + 1) * LANES] = (pair.T * gate).astype(BF16)

    if has_local:
        def emit(ci, o):
            of_ref[0, ci * GLA_CHUNK:(ci + 1) * GLA_CHUNK, :] = o

        def final_state(S):
            st_ref[...] = S

            @pl.when(t == nt - 1)
            def _():
                sfin_ref[0] = S

        scan = _gla_scan_steps(gl_ref, gv_ref, dec_ref, st_ref[...], False, nsub * A, emit, final_state)
    else:
        scan = iter(())

    units = [(j, g) for j in range(nsub) for g in range(ATTN_KV_HEADS)]
    ahead = min(ATTN_LOOKAHEAD, len(units))
    pending = [scores(*u) for u in units[:ahead]]
    for n, unit in enumerate(units):
        if n + ahead < len(units):
            pending.append(scores(*units[n + ahead]))
        next(scan, None)
        finish(*unit, *pending.pop(0))
    for _ in scan:
        pass


def _attn_caps():
    A = ATTN_BLOCK
    c = np.arange(3 * A)[:, None]
    q = np.arange(A)[None, :]
    band = (c >= q) & (c <= q + 2 * A)
    caps = [band, band & (c >= A), band & (c < 2 * A)]
    return jnp.asarray(np.stack([np.where(b, PASS_CAP, NEG_INF) for b in caps]), F32)


def _attention(q, kk, vt, ckk, cvt, gb, sink, tq, gla=None):
    B, S, _ = q.shape
    nt = S // tq
    nsub = tq // ATTN_BLOCK
    nblk = S // ATTN_BLOCK
    nch = tq // GLA_CHUNK
    A = ATTN_BLOCK
    nc = ckk.shape[1]
    has_local = gla is not None
    assert not has_local or nblk >= 2
    caps = _attn_caps()
    prev = lambda t: jnp.maximum(t * nsub - 1, 0)
    nxt = lambda t: jnp.minimum((t + 1) * nsub, nblk - 1)
    tok = lambda w: pl.BlockSpec((1, tq, w), lambda b, t: (b, t, 0))
    st_spec = pl.BlockSpec((1, GLA_WIDTH, GLA_KW), lambda b, t: (b, 0, 0))
    in_specs = [pl.BlockSpec(memory_space=pltpu.SMEM),
                tok(ATTN_WIDTH), tok(256),
                pl.BlockSpec((1, A, 256), lambda b, t: (b, prev(t), 0)),
                pl.BlockSpec((1, A, 256), lambda b, t: (b, nxt(t), 0)),
                pl.BlockSpec((1, LANES, tq), lambda b, t: (b, 0, t)),
                pl.BlockSpec((1, LANES, A), lambda b, t: (b, 0, prev(t))),
                pl.BlockSpec((1, LANES, A), lambda b, t: (b, 0, nxt(t))),
                pl.BlockSpec((1, nc, 256), lambda b, t: (b, 0, 0)),
                pl.BlockSpec((1, LANES, nc), lambda b, t: (b, 0, 0)),
                pl.BlockSpec(caps.shape, lambda b, t: (0, 0, 0)),
                tok(ATTN_WIDTH)]
    args = [sink, q, kk, kk, kk, vt, vt, vt, ckk, cvt, caps, gb]
    out_specs = [tok(ATTN_WIDTH)]
    out_shape = [jax.ShapeDtypeStruct((B, S, ATTN_WIDTH), BF16)]
    scratch = [pltpu.VMEM((tq + 2 * A, 256), BF16), pltpu.VMEM((nsub + 2, LANES, A), BF16)]
    if has_local:
        in_specs += [tok(384), tok(256), pl.BlockSpec((1, nch, 2 * GLA_KW), lambda b, t: (b, t, 0)), st_spec]
        args += list(gla)
        out_specs += [tok(GLA_WIDTH), st_spec]
        out_shape += [jax.ShapeDtypeStruct((B, S, GLA_WIDTH), F32),
                      jax.ShapeDtypeStruct((B, GLA_WIDTH, GLA_KW), F32)]
        scratch.append(pltpu.VMEM((GLA_WIDTH, GLA_KW), F32))
    out = pl.pallas_call(
        functools.partial(_attn_kernel, nsub=nsub, nblk=nblk, nt=nt, has_local=has_local),
        grid=(B, nt),
        in_specs=in_specs,
        out_specs=out_specs,
        out_shape=out_shape,
        scratch_shapes=scratch,
        compiler_params=_cparams(("parallel", "arbitrary"), 40),
        name="attention_gla_fwd" if has_local else "ctx_attention",
    )(*args)
    return out if has_local else out[0]


def _gla_kernel(*refs, reverse, tg, nt):
    if reverse:
        gl_ref, v_ref, dec_ref, s0_ref, of_ref, gc_ref, nw_ref, o_ref, sfin_ref, st_ref = refs
        head_mean = _head_mean_matrix()
    else:
        gl_ref, v_ref, dec_ref, s0_ref, o_ref, sfin_ref, st_ref = refs
    t = pl.program_id(1)
    C = GLA_CHUNK

    @pl.when(t == 0)
    def _():
        st_ref[...] = s0_ref[0]

    def emit(ci, o):
        r0 = ci * C
        if reverse:
            ot = o + of_ref[0, r0:r0 + C, :]
            o_ref[0, r0:r0 + C, :] = _gla_norm_gate(ot, head_mean, nw_ref[...], gc_ref[0, r0:r0 + C, :])
        else:
            o_ref[0, r0:r0 + C, :] = o

    S = _gla_scan(gl_ref, v_ref, dec_ref, st_ref[...], reverse, tg, emit)
    st_ref[...] = S

    @pl.when(t == nt - 1)
    def _():
        sfin_ref[0] = S


def _gla_pass(gl, v, dec, s0, reverse, tg, extra=None):
    B, S, _ = gl.shape
    nt = S // tg
    nch = tg // GLA_CHUNK
    if reverse:
        tmap = lambda b, t: (b, nt - 1 - t, 0)
    else:
        tmap = lambda b, t: (b, t, 0)
    tok = lambda w: pl.BlockSpec((1, tg, w), tmap)
    st_spec = pl.BlockSpec((1, GLA_WIDTH, GLA_KW), lambda b, t: (b, 0, 0))
    in_specs = [tok(384), tok(256), pl.BlockSpec((1, nch, 2 * GLA_KW), tmap), st_spec]
    args = [gl, v, dec, s0]
    if reverse:
        of, gc, nw = extra
        in_specs += [tok(256), tok(256), pl.BlockSpec((1, GLA_WIDTH), lambda b, t: (0, 0))]
        args += [of, gc, nw.reshape(1, GLA_WIDTH)]
    return pl.pallas_call(
        functools.partial(_gla_kernel, reverse=reverse, tg=tg, nt=nt),
        grid=(B, nt),
        in_specs=in_specs,
        out_specs=[tok(256), st_spec],
        out_shape=[jax.ShapeDtypeStruct((B, S, GLA_WIDTH), BF16 if reverse else F32),
                   jax.ShapeDtypeStruct((B, GLA_WIDTH, GLA_KW), F32)],
        scratch_shapes=[pltpu.VMEM((GLA_WIDTH, GLA_KW), F32)],
        compiler_params=_cparams(("parallel", "arbitrary"), 32),
        name="gla_bwd" if reverse else "gla_fwd",
    )(*args)


def _tail_kernel(gl_ref, v_ref, dec_ref, s0_ref, of_ref, gc_ref, nw_ref,
                 um_ref, up_ref, un_ref, ga_ref, cw_ref, cb_ref, lnw_ref, lnb_ref,
                 b_ref, x_ref, mod_ref, w_ref, fw_ref,
                 o_ref, sfin_ref, st_ref, buf_ref, sh_ref, *, ts, nt, final):
    t = pl.program_id(1)
    C = GLA_CHUNK
    nchunk = ts // C
    part = max(nchunk // TAIL_PARTS, 1)

    @pl.when(t == 0)
    def _():
        st_ref[...] = s0_ref[0]

    _conv_prepare(t == nt - 1, t == 0, um_ref, up_ref, un_ref, buf_ref, sh_ref, ts)
    yb = _dot(b_ref[0], w_ref[256:768, :])
    head_mean = _head_mean_matrix()
    gate = mod_ref[:, 2 * D_MODEL:3 * D_MODEL]
    a_blk, c_blk = {}, {}

    def emit(ci, o):
        r0 = ci * C
        ot = o + of_ref[0, r0:r0 + C, :]
        c_blk[ci] = _gla_norm_gate(ot, head_mean, nw_ref[...], gc_ref[0, r0:r0 + C, :])
        a_blk[ci] = _conv_rows(r0, C, sh_ref, ga_ref, cw_ref, cb_ref, lnw_ref, lnb_ref)
        if ci % part == 0:
            grp = range(ci, ci + part)
            rows = slice(ci * C, (ci + part) * C)
            y = (yb[rows] + _dot(jnp.concatenate([a_blk[i] for i in grp], axis=0), w_ref[0:256, :])
                 + _dot(jnp.concatenate([c_blk[i] for i in grp], axis=0), w_ref[768:1024, :]))
            xn = x_ref[0, rows, :] + gate * y
            if final:
                ms = jnp.mean(xn * xn, axis=-1, keepdims=True)
                xn = xn * lax.rsqrt(ms + NORM_EPS) * fw_ref[...]
            o_ref[0, rows, :] = xn

    S = _gla_scan(gl_ref, v_ref, dec_ref, st_ref[...], True, ts, emit)
    st_ref[...] = S

    @pl.when(t == nt - 1)
    def _():
        sfin_ref[0] = S


def _layer_tail(glb, gv, dec, s0b, of, gc, norm_w, u, ga, conv_w, conv_b, ln_w, ln_b, b, x, mod4, layer, mod_row,
                w_out, final_w, ts, final):
    B, S, _ = x.shape
    nt = S // ts
    nch = ts // GLA_CHUNK
    hb = ts // CONV_HALO
    nhb = S // CONV_HALO
    n_sh = ts + 2 * CONV_HALO - SUBLANES
    tile = lambda t: nt - 1 - t
    if mod_row is None:
        mod_map = lambda bb, t: (layer, bb, 0, 0)
    else:
        mod_map = lambda bb, t: (layer, mod_row, 0, 0)
    tok = lambda w: pl.BlockSpec((1, ts, w), lambda bb, t: (bb, tile(t), 0))
    st_spec = pl.BlockSpec((1, GLA_WIDTH, GLA_KW), lambda bb, t: (bb, 0, 0))
    row = lambda a: a.reshape(1, -1)
    vec = lambda w: pl.BlockSpec((1, w), lambda bb, t: (0, 0))
    halo_prev = pl.BlockSpec((1, CONV_HALO, CONV_WIDTH), lambda bb, t: (bb, jnp.maximum(tile(t) * hb - 1, 0), 0))
    halo_next = pl.BlockSpec((1, CONV_HALO, CONV_WIDTH),
                             lambda bb, t: (bb, jnp.minimum((tile(t) + 1) * hb, nhb - 1), 0))
    return pl.pallas_call(
        functools.partial(_tail_kernel, ts=ts, nt=nt, final=final),
        grid=(B, nt),
        in_specs=[tok(384), tok(256), pl.BlockSpec((1, nch, 2 * GLA_KW), lambda bb, t: (bb, tile(t), 0)), st_spec,
                  tok(256), tok(256), vec(GLA_WIDTH),
                  tok(CONV_WIDTH), halo_prev, halo_next, tok(CONV_WIDTH),
                  pl.BlockSpec((CONV_KERNEL, CONV_WIDTH), lambda bb, t: (0, 0)),
                  vec(CONV_WIDTH), vec(CONV_WIDTH), vec(CONV_WIDTH),
                  tok(ATTN_WIDTH), tok(D_MODEL),
                  pl.BlockSpec((None, None, 1, 3 * D_MODEL), mod_map),
                  pl.BlockSpec((D_MODEL, D_MODEL), lambda bb, t: (0, 0)),
                  vec(D_MODEL)],
        out_specs=[tok(D_MODEL), st_spec],
        out_shape=[jax.ShapeDtypeStruct((B, S, D_MODEL), F32),
                   jax.ShapeDtypeStruct((B, GLA_WIDTH, GLA_KW), F32)],
        scratch_shapes=[pltpu.VMEM((GLA_WIDTH, GLA_KW), F32),
                        pltpu.VMEM((ts + 2 * CONV_HALO, CONV_WIDTH), F32),
                        pltpu.VMEM((SUBLANES, n_sh, CONV_WIDTH), F32)],
        compiler_params=_cparams(("parallel", "arbitrary"), 48),
        name="layer_tail",
    )(glb, gv, dec, s0b, of, gc, row(norm_w), u, u, u, ga, conv_w, row(conv_b), row(ln_w), row(ln_b),
      b, x, mod4, w_out, row(final_w))


def _rope_tables(n_tokens):
    rows = n_tokens // GRID_W
    r = jnp.repeat(jnp.arange(rows, dtype=F32), GRID_W)
    col = jnp.tile(jnp.arange(GRID_W, dtype=F32), rows)
    n_freq = ATTN_HEAD_DIM // 4
    inv = ROPE_THETA ** (-jnp.arange(n_freq, dtype=F32) / n_freq)
    ang = jnp.concatenate([r[:, None] * inv, col[:, None] * inv], axis=-1)
    cos = jnp.tile(jnp.cos(ang), (1, 4))
    sin = jnp.sin(ang)
    return cos, jnp.concatenate([-sin, -sin, sin, sin], axis=-1)


def _split_heads(w, n_pairs):
    lead = w.shape[:-1]
    w = w.reshape(lead + (n_pairs, 2, 2, 32))
    return jnp.swapaxes(w, -2, -3).reshape(lead + (n_pairs * LANES,))


def _relayout_w_in(w_in):
    pad = jnp.zeros(w_in.shape[:-1] + (N_IN_PAD - N_IN,), F32)
    return jnp.concatenate([
        w_in[..., 0:768],
        _split_heads(w_in[..., 768:1280], 4),
        _split_heads(w_in[..., 1280:1408], 1),
        w_in[..., 1408:2560],
        w_in[..., 2592:2848],
        w_in[..., 2560:2592],
        pad], axis=-1).astype(BF16)


def _cumsum_matrices():
    n = GLA_GROUP // GLA_CHUNK
    low = np.kron(np.eye(n), np.tril(np.ones((GLA_CHUNK, GLA_CHUNK))))
    return jnp.asarray(np.stack([low, low.T]), BF16)


def kernel(x, c, ctx, c_ctx, w_mod, b_mod, w_in, conv_w, conv_b, conv_ln_w, conv_ln_b, attn_sink,
           gla_w_up, gla_b_up, gla_norm_w, w_out, final_norm_w):
    B, S, D = x.shape
    NC = ctx.shape[1]
    TS = TOKEN_TILE

    w_p = _relayout_w_in(w_in)
    w_o = w_out.astype(BF16)
    w_up2 = jnp.zeros((DEPTH, LANES, 256), F32)
    w_up2 = w_up2.at[:, 0:GLA_LOW_RANK, 0:128].set(gla_w_up[:, 0])
    w_up2 = w_up2.at[:, GLA_LOW_RANK:2 * GLA_LOW_RANK, 128:256].set(gla_w_up[:, 1])
    wuh = w_up2.astype(BF16)
    wul = (w_up2 - wuh.astype(F32)).astype(BF16)
    bup = gla_b_up.reshape(DEPTH, 1, 256)
    tri = _cumsum_matrices()
    cos_l, sin_l = _rope_tables(S)
    cos_c = jnp.ones((NC, LANES), F32)
    sin_c = jnp.zeros((NC, LANES), F32)
    zero_state = jnp.zeros((B, GLA_WIDTH, GLA_KW), F32)

    cc = jnp.concatenate([c, c_ctx[None, :], jnp.zeros((16 - B - 1, D), F32)], axis=0)
    mod4 = _modulation(cc, w_mod, b_mod).reshape(DEPTH, 16, 1, 3 * D)
    CTX_ROW = B

    xl, xc = x, ctx
    for i in range(DEPTH):
        last = i == DEPTH - 1
        conv_p = (conv_w[i], conv_b[i], conv_ln_w[i], conv_ln_b[i])
        (u_c, q_c, kk_c, vt_c, ga_c, gb_c, gc_c, glf_c, glb_c, gv_c, dec_c) = _project(
            xc, mod4, i, CTX_ROW, cos_c, sin_c, w_p[i], wuh[i], wul[i], bup[i], tri, NC)
        of_c, s_f = _gla_pass(glf_c, gv_c, dec_c, zero_state, False, NC)
        if last:
            _, s_b = _gla_pass(glb_c, gv_c, dec_c, zero_state, True, NC, extra=(of_c, gc_c, gla_norm_w[i]))
        else:
            b_c = _attention(q_c, kk_c, vt_c, kk_c, vt_c, gb_c, attn_sink[i], NC)
            xc, s_b = _layer_tail(glb_c, gv_c, dec_c, zero_state, of_c, gc_c, gla_norm_w[i], u_c, ga_c, *conv_p,
                                  b_c, xc, mod4, i, CTX_ROW, w_o[i], final_norm_w, NC, False)
        (u_l, q_l, kk_l, vt_l, ga_l, gb_l, gc_l, glf_l, glb_l, gv_l, dec_l) = _project(
            xl, mod4, i, None, cos_l, sin_l, w_p[i], wuh[i], wul[i], bup[i], tri, TS)
        b_l, of_l, _ = _attention(q_l, kk_l, vt_l, kk_c, vt_c, gb_l, attn_sink[i], TS,
                                  gla=(glf_l, gv_l, dec_l, s_f))
        xl, _ = _layer_tail(glb_l, gv_l, dec_l, s_b, of_l, gc_l, gla_norm_w[i], u_l, ga_l, *conv_p,
                            b_l, xl, mod4, i, None, w_o[i], final_norm_w, TS, last)
    return xl
```
